```python
import math
import jax, jax.numpy as jnp
from jax import lax
import numpy as np

D_MODEL = 2048
BATCH = 2
SEQ = 16384
DEPTH = 4

N_MIXERS = 2
N_LAYERS_A = (DEPTH + 1) // 2
N_LAYERS_B = DEPTH // 2

DSA_PATTERNS = ((128, 1), (512, 4), (2048, 16))
DSA_GROUPS = len(DSA_PATTERNS)
DSA_HEADS = 16
DSA_HEAD_DIM = 128
DSA_WIDTH = DSA_HEADS * DSA_HEAD_DIM
DSA_BLOCK = 128
DSA_IN = 3 * DSA_GROUPS * DSA_WIDTH + DSA_WIDTH

MLA_HEADS = 16
MLA_Q_RANK = 512
MLA_KV_RANK = 512
MLA_NOPE = 128
MLA_ROPE = 64
MLA_V = 128
MLA_WIDTH = MLA_HEADS * MLA_V
MLA_IN = MLA_Q_RANK + MLA_KV_RANK + MLA_ROPE + MLA_WIDTH
MLA_QBLOCK = 128

ROPE_THETA = 10000.0
RMS_EPS = 1e-6
LN_EPS = 1e-5
DEEPNORM_ALPHA = (2 * DEPTH) ** 0.25
DEEPNORM_BETA = (8 * DEPTH) ** -0.25

kernel_name = "hybrid_dilated_mla_deepnorm"


def rope_tables(positions, dim):
    inv = 1.0 / (ROPE_THETA ** (jnp.arange(0, dim, 2, dtype=jnp.float32) / dim))
    ang = positions.astype(jnp.float32)[..., None] * inv
    return jnp.cos(ang), jnp.sin(ang)


def apply_rope(x, cos, sin):
    xf = x.astype(jnp.float32)
    x1, x2 = jnp.split(xf, 2, axis=-1)
    c = cos[:, :, None, :]
    s = sin[:, :, None, :]
    return jnp.concatenate([x1 * c - x2 * s, x2 * c + x1 * s], axis=-1).astype(x.dtype)


def rms_norm(x, g):
    xf = x.astype(jnp.float32)
    y = xf * lax.rsqrt(jnp.mean(xf * xf, axis=-1, keepdims=True) + RMS_EPS)
    return (y * g.astype(jnp.float32)).astype(x.dtype)


def layer_norm(x, g, b):
    xf = x.astype(jnp.float32)
    mu = jnp.mean(xf, axis=-1, keepdims=True)
    xc = xf - mu
    var = jnp.mean(xc * xc, axis=-1, keepdims=True)
    y = xc * lax.rsqrt(var + LN_EPS) * g.astype(jnp.float32) + b.astype(jnp.float32)
    return y.astype(x.dtype)


def dilated_window_attention(q, k, v, window, dilation):
    B, S, H, Dh = q.shape
    span = window // dilation
    assert span <= DSA_BLOCK
    L = S // dilation
    nb = -(-L // DSA_BLOCK)
    Lp = nb * DSA_BLOCK
    N = B * dilation

    def to_sub(t):
        t = t.reshape(B, L, dilation, H, Dh).transpose(0, 2, 1, 3, 4).reshape(N, L, H, Dh)
        t = jnp.pad(t, ((0, 0), (0, Lp - L), (0, 0), (0, 0)))
        return t.reshape(N, nb, DSA_BLOCK, H, Dh)

    def with_prev(t):
        prev = jnp.pad(t, ((0, 0), (1, 0), (0, 0), (0, 0), (0, 0)))[:, :-1]
        return jnp.concatenate([prev, t], axis=2)

    qb = to_sub(q)
    kk = with_prev(to_sub(k))
    vv = with_prev(to_sub(v))
    scores = jnp.einsum('nbqhd,nbkhd->nbhqk', qb, kk).astype(jnp.float32) * (Dh ** -0.5)
    qi = jnp.arange(nb)[:, None, None] * DSA_BLOCK + jnp.arange(DSA_BLOCK)[None, :, None]
    kj = (jnp.arange(nb)[:, None, None] - 1) * DSA_BLOCK + jnp.arange(2 * DSA_BLOCK)[None, None, :]
    dist = qi - kj
    mask = (dist >= 0) & (dist <= span) & (kj >= 0)
    scores = jnp.where(mask[None, :, None], scores, -jnp.inf)
    lse = jax.nn.logsumexp(scores, axis=-1)
    p = jnp.exp(scores - lse[..., None])
    out = jnp.einsum('nbhqk,nbkhd->nbqhd', p.astype(v.dtype), vv)
    out = out.reshape(B, dilation, Lp, H, Dh)[:, :, :L].transpose(0, 2, 1, 3, 4).reshape(B, S, H, Dh)
    lse = lse.transpose(0, 1, 3, 2).reshape(B, dilation, Lp, H)[:, :, :L]
    lse = lse.transpose(0, 2, 1, 3).reshape(B, S, H)
    return out, lse


def dilated_mixer(x, w_in, w_out, cos, sin):
    B, S, _ = x.shape
    h = x @ w_in
    qkv = h[..., :3 * DSA_GROUPS * DSA_WIDTH].reshape(B, S, DSA_GROUPS, 3, DSA_HEADS, DSA_HEAD_DIM)
    z = h[..., 3 * DSA_GROUPS * DSA_WIDTH:]
    outs = []
    lses = []
    for g, (window, dilation) in enumerate(DSA_PATTERNS):
        q = apply_rope(qkv[:, :, g, 0], cos, sin)
        k = apply_rope(qkv[:, :, g, 1], cos, sin)
        o, l = dilated_window_attention(q, k, qkv[:, :, g, 2], window, dilation)
        outs.append(o)
        lses.append(l)
    wts = jax.nn.softmax(jnp.stack(lses, axis=0), axis=0)
    o = jnp.einsum('gbsh,gbshd->bshd', wts, jnp.stack(outs, axis=0).astype(jnp.float32))
    y = o.reshape(B, S, DSA_WIDTH).astype(x.dtype) * jax.nn.silu(z)
    return y @ w_out


def causal_mla_attention(q_nope, q_pe, k_nope, k_pe, v):
    B, S, H, _ = q_nope.shape
    nq = S // MLA_QBLOCK
    scale = (MLA_NOPE + MLA_ROPE) ** -0.5
    qn = q_nope.reshape(B, nq, MLA_QBLOCK, H, MLA_NOPE).transpose(1, 0, 2, 3, 4)
    qp = q_pe.reshape(B, nq, MLA_QBLOCK, H, MLA_ROPE).transpose(1, 0, 2, 3, 4)
    kpos = jnp.arange(S)

    def one_block(args):
        i, qn_i, qp_i = args
        s = (jnp.einsum('bqhd,bkhd->bhqk', qn_i, k_nope).astype(jnp.float32)
             + jnp.einsum('bqhr,bkr->bhqk', qp_i, k_pe).astype(jnp.float32)) * scale
        qpos = i * MLA_QBLOCK + jnp.arange(MLA_QBLOCK)
        s = jnp.where(kpos[None, :] <= qpos[:, None], s, -jnp.inf)
        p = jax.nn.softmax(s, axis=-1)
        return jnp.einsum('bhqk,bkhd->bqhd', p.astype(v.dtype), v)

    o = lax.map(one_block, (jnp.arange(nq), qn, qp))
    return o.transpose(1, 0, 2, 3, 4).reshape(B, S, H, MLA_V)


def mla_mixer(x, w_in, q_norm, w_uq, kv_norm, w_ukv, w_out, cos, sin):
    B, S, _ = x.shape
    h = x @ w_in
    o1 = MLA_Q_RANK
    o2 = o1 + MLA_KV_RANK
    o3 = o2 + MLA_ROPE
    c_q = h[..., :o1]
    c_kv = h[..., o1:o2]
    k_pe = h[..., o2:o3]
    z = h[..., o3:]
    q = (rms_norm(c_q, q_norm) @ w_uq).reshape(B, S, MLA_HEADS, MLA_NOPE + MLA_ROPE)
    q_nope = q[..., :MLA_NOPE]
    q_pe = apply_rope(q[..., MLA_NOPE:], cos, sin)
    k_pe = apply_rope(k_pe[:, :, None, :], cos, sin)[:, :, 0]
    kv = (rms_norm(c_kv, kv_norm) @ w_ukv).reshape(B, S, MLA_HEADS, MLA_NOPE + MLA_V)
    k_nope = kv[..., :MLA_NOPE]
    v = kv[..., MLA_NOPE:]
    o = causal_mla_attention(q_nope, q_pe, k_nope, k_pe, v)
    y = o.reshape(B, S, MLA_WIDTH) * jax.nn.silu(z)
    return y @ w_out


def setup_inputs(seed: int = 0) -> dict:
    key = jax.random.key(seed)
    ks = jax.random.split(key, 12)
    f32 = jnp.float32
    x = jax.random.normal(ks[0], (BATCH, SEQ, D_MODEL), f32)
    start = jax.random.randint(ks[1], (BATCH, 1), 0, 4096, dtype=jnp.int32)
    positions = (start + jnp.arange(SEQ, dtype=jnp.int32)[None, :]).astype(jnp.int32)
    dsa_w_in = jax.random.normal(ks[2], (N_LAYERS_A, D_MODEL, DSA_IN), f32) * D_MODEL ** -0.5
    dsa_w_out = jax.random.normal(ks[3], (N_LAYERS_A, DSA_WIDTH, D_MODEL), f32) * (DSA_WIDTH ** -0.5 * DEEPNORM_BETA)
    mla_w_in = jax.random.normal(ks[4], (N_LAYERS_B, D_MODEL, MLA_IN), f32) * D_MODEL ** -0.5
    mla_q_norm = 1.0 + 0.01 * jax.random.normal(ks[5], (N_LAYERS_B, MLA_Q_RANK), f32)
    mla_w_uq = jax.random.normal(ks[6], (N_LAYERS_B, MLA_Q_RANK, MLA_HEADS * (MLA_NOPE + MLA_ROPE)), f32) * MLA_Q_RANK ** -0.5
    mla_kv_norm = 1.0 + 0.01 * jax.random.normal(ks[7], (N_LAYERS_B, MLA_KV_RANK), f32)
    mla_w_ukv = jax.random.normal(ks[8], (N_LAYERS_B, MLA_KV_RANK, MLA_HEADS * (MLA_NOPE + MLA_V)), f32) * MLA_KV_RANK ** -0.5
    mla_w_out = jax.random.normal(ks[9], (N_LAYERS_B, MLA_WIDTH, D_MODEL), f32) * (MLA_WIDTH ** -0.5 * DEEPNORM_BETA)
    ln_g = 1.0 + 0.01 * jax.random.normal(ks[10], (DEPTH, D_MODEL), f32)
    ln_b = 0.01 * jax.random.normal(ks[11], (DEPTH, D_MODEL), f32)
    return {"x": x, "positions": positions, "dsa_w_in": dsa_w_in, "dsa_w_out": dsa_w_out,
            "mla_w_in": mla_w_in, "mla_q_norm": mla_q_norm, "mla_w_uq": mla_w_uq,
            "mla_kv_norm": mla_kv_norm, "mla_w_ukv": mla_w_ukv, "mla_w_out": mla_w_out,
            "ln_g": ln_g, "ln_b": ln_b}


def reference(x, positions, dsa_w_in, dsa_w_out, mla_w_in, mla_q_norm, mla_w_uq,
              mla_kv_norm, mla_w_ukv, mla_w_out, ln_g, ln_b):
    cos_a, sin_a = rope_tables(positions, DSA_HEAD_DIM)
    cos_b, sin_b = rope_tables(positions, MLA_ROPE)
    for layer in range(DEPTH):
        j = layer // N_MIXERS
        if layer % N_MIXERS == 0:
            y = dilated_mixer(x, dsa_w_in[j], dsa_w_out[j], cos_a, sin_a)
        else:
            y = mla_mixer(x, mla_w_in[j], mla_q_norm[j], mla_w_uq[j], mla_kv_norm[j],
                          mla_w_ukv[j], mla_w_out[j], cos_b, sin_b)
        x = layer_norm(DEEPNORM_ALPHA * x + y, ln_g[layer], ln_b[layer])
    return x
```

```python
import functools

import jax
import jax.numpy as jnp
import numpy as np
from jax import lax
from jax.experimental import pallas as pl
from jax.experimental.pallas import tpu as pltpu

F32 = jnp.float32
BF16 = jnp.bfloat16

D_MODEL = 2048
DEPTH = 4
N_MIXERS = 2

DSA_PATTERNS = ((128, 1), (512, 4), (2048, 16))
DSA_GROUPS = len(DSA_PATTERNS)
DSA_HEADS = 16
DSA_HEAD_DIM = 128
DSA_WIDTH = DSA_HEADS * DSA_HEAD_DIM
DSA_BLOCK = 128

MLA_HEADS = 16
MLA_Q_RANK = 512
MLA_KV_RANK = 512
MLA_NOPE = 128
MLA_ROPE = 64
MLA_V = 128
MLA_WIDTH = MLA_HEADS * MLA_V

ROPE_THETA = 10000.0
RMS_EPS = 1e-6
LN_EPS = 1e-5
DEEPNORM_ALPHA = (2 * DEPTH) ** 0.25

LANES = 128
MIB = 1024 * 1024

DSA_TILE = 2048
DSA_HEADS_PER_STEP = 2
MLA_TILE = 1024
MLA_HEADS_PER_STEP = 2
MLA_QK = 2 * LANES


def _params(semantics, vmem_mib):
    return pltpu.CompilerParams(dimension_semantics=semantics, vmem_limit_bytes=vmem_mib * MIB)


def _rope_table_kernel(pos_ref, inv_ref, mult_ref, *out_refs, kinds):
    ang = pos_ref[...] * inv_ref[...]
    c = jnp.cos(ang)
    s = jnp.sin(ang)
    for i, (kind, o_ref) in enumerate(zip(kinds, out_refs)):
        o_ref[...] = (c if kind == "cos" else s) * mult_ref[i:i + 1, :]


def _rope_tables(pos_f32, inv_row, mult_rows, kinds, tm=2048):
    n = pos_f32.shape[0]
    nk = len(kinds)
    return pl.pallas_call(
        functools.partial(_rope_table_kernel, kinds=kinds),
        grid=(n // tm,),
        in_specs=[pl.BlockSpec((tm, 1), lambda i: (i, 0)),
                  pl.BlockSpec((1, LANES), lambda i: (0, 0)),
                  pl.BlockSpec((nk, LANES), lambda i: (0, 0))],
        out_specs=[pl.BlockSpec((tm, LANES), lambda i: (i, 0))] * nk,
        out_shape=[jax.ShapeDtypeStruct((n, LANES), F32)] * nk,
        compiler_params=_params(("arbitrary",), 32),
        name="rope_tables",
    )(pos_f32, inv_row, mult_rows)


def _cast_kernel(x_ref, o_ref):
    o_ref[0, 0] = x_ref[0].astype(BF16)


def _cast_perm(x3, d, tm=512):
    b, s, dm = x3.shape
    length = s // d
    tm = min(tm, length)
    out = pl.pallas_call(
        _cast_kernel,
        grid=(b, d, length // tm),
        in_specs=[pl.BlockSpec((1, tm, dm), lambda bi, r, t: (bi, t, r))],
        out_specs=pl.BlockSpec((1, 1, tm, dm), lambda bi, r, t: (bi, r, t, 0)),
        out_shape=jax.ShapeDtypeStruct((b, d, length, dm), BF16),
        compiler_params=_params(("arbitrary",) * 3, 32),
        name=f"cast_perm_d{d}",
    )(x3.reshape(b, length, d * dm))
    return out.reshape(b * s, dm)


def _mm_kernel(a_ref, w_ref, o_ref):
    o_ref[...] = jnp.dot(a_ref[...], w_ref[...], preferred_element_type=F32).astype(o_ref.dtype)


def _mm(a, w, out_dtype=BF16, tm=1024, tn=1024):
    m, k = a.shape
    n = w.shape[1]
    return pl.pallas_call(
        _mm_kernel,
        grid=(m // tm, n // tn),
        in_specs=[pl.BlockSpec((tm, k), lambda i, j: (i, 0)),
                  pl.BlockSpec((k, tn), lambda i, j: (0, j))],
        out_specs=pl.BlockSpec((tm, tn), lambda i, j: (i, j)),
        out_shape=jax.ShapeDtypeStruct((m, n), out_dtype),
        compiler_params=_params(("arbitrary", "arbitrary"), 48),
        name="matmul",
    )(a, w)


def _mm_rope_kernel(a_ref, w_ref, cos_ref, sin_ref, o_ref, *, n_q_blocks, n_rope_blocks, q_scale):
    acc = jnp.dot(a_ref[...], w_ref[...], preferred_element_type=F32)
    j = pl.program_id(1)

    @pl.when(j < n_rope_blocks)
    def _():
        scale = jnp.where(j < n_q_blocks, q_scale, 1.0).astype(F32)
        cos = cos_ref[...]
        sin = sin_ref[...] * scale
        cos = cos * scale
        for c in range(acc.shape[1] // LANES):
            xs = acc[:, c * LANES:(c + 1) * LANES]
            o_ref[:, c * LANES:(c + 1) * LANES] = (
                xs * cos + pltpu.roll(xs, LANES // 2, 1) * sin).astype(o_ref.dtype)

    @pl.when(j >= n_rope_blocks)
    def _():
        o_ref[...] = acc.astype(o_ref.dtype)


def _mm_rope(a, w, cos, sin, q_scale, tm=1024, tn=1024):
    m, k = a.shape
    n = w.shape[1]
    kern = functools.partial(_mm_rope_kernel, n_q_blocks=DSA_WIDTH // tn,
                             n_rope_blocks=2 * DSA_WIDTH // tn, q_scale=q_scale)
    return pl.pallas_call(
        kern,
        grid=(m // tm, n // tn),
        in_specs=[pl.BlockSpec((tm, k), lambda i, j: (i, 0)),
                  pl.BlockSpec((k, tn), lambda i, j: (0, j)),
                  pl.BlockSpec((tm, LANES), lambda i, j: (i, 0)),
                  pl.BlockSpec((tm, LANES), lambda i, j: (i, 0))],
        out_specs=pl.BlockSpec((tm, tn), lambda i, j: (i, j)),
        out_shape=jax.ShapeDtypeStruct((m, n), BF16),
        compiler_params=_params(("arbitrary", "arbitrary"), 48),
        name="matmul_rope",
    )(a, w, cos, sin)


def _band_attention(q, kwin, vwin, mask):
    s = lax.dot_general(q, kwin, (((1,), (1,)), ((), ())), preferred_element_type=F32)
    s = jnp.where(mask, s, -jnp.inf)
    m = jnp.max(s, axis=1, keepdims=True)
    p = jnp.exp(s - m)
    l = jnp.sum(p, axis=1, keepdims=True)
    o = jnp.dot(p.astype(BF16), vwin, preferred_element_type=F32) * (1.0 / l)
    lse = m + jnp.log(l)
    return o, jnp.broadcast_to(lse, o.shape)


def _dsa_attn_kernel(*refs, dilations, spans, heads):
    ng = len(dilations)
    qkv_refs = refs[:3 * ng]
    z_ref, o_ref = refs[3 * ng], refs[3 * ng + 1]
    carry_refs = refs[3 * ng + 2:3 * ng + 2 + 2 * ng]
    o_scr, l_scr = refs[3 * ng + 2 + 2 * ng:]
    blk = DSA_BLOCK
    tile = z_ref.shape[1]
    t = pl.program_id(2)

    @pl.when(t == 0)
    def _():
        for c in carry_refs:
            c[...] = jnp.zeros(c.shape, c.dtype)

    has_prev = t > 0
    row = lax.broadcasted_iota(jnp.int32, (blk, 2 * blk), 0)
    col = lax.broadcasted_iota(jnp.int32, (blk, 2 * blk), 1)

    for g in range(ng):
        d = dilations[g]
        q_ref, k_ref, v_ref = qkv_refs[3 * g:3 * g + 3]
        kp_ref, vp_ref = carry_refs[2 * g:2 * g + 2]
        nb = tile // d // blk
        dist = row + blk - col
        band = (dist >= 0) & (dist <= spans[g])
        band_first = band & ((col >= blk) | has_prev)

        def store(start, hh, o, lse, g=g, d=d):
            idx = pl.ds(start, blk) if d == 1 else pl.ds(start, blk, stride=d)
            o_scr[hh, g, idx, :] = o
            l_scr[hh, g, idx, :] = lse

        def residue(r, carry, g=g, d=d, q_ref=q_ref, k_ref=k_ref, v_ref=v_ref,
                    kp_ref=kp_ref, vp_ref=vp_ref, nb=nb, band=band, band_first=band_first, store=store):
            for hh in range(heads):
                ln = slice(hh * LANES, (hh + 1) * LANES)
                q = q_ref[0, r, 0:blk, ln]
                kwin = jnp.concatenate([kp_ref[r, :, ln], k_ref[0, r, 0:blk, ln]], axis=0)
                vwin = jnp.concatenate([vp_ref[r, :, ln], v_ref[0, r, 0:blk, ln]], axis=0)
                o, lse = _band_attention(q, kwin, vwin, band_first)
                store(r, hh, o, lse)

            def block(bb, c2):
                r0 = pl.multiple_of(bb * blk, blk)
                for hh in range(heads):
                    ln = slice(hh * LANES, (hh + 1) * LANES)
                    q = q_ref[0, r, pl.ds(r0, blk), ln]
                    kwin = k_ref[0, r, pl.ds(r0 - blk, 2 * blk), ln]
                    vwin = v_ref[0, r, pl.ds(r0 - blk, 2 * blk), ln]
                    o, lse = _band_attention(q, kwin, vwin, band)
                    store(r0 * d + r, hh, o, lse)
                return c2

            if nb > 1:
                lax.fori_loop(1, nb, block, 0)
            return carry

        if d == 1:
            residue(0, 0)
        else:
            lax.fori_loop(0, d, residue, 0)

    for g in range(ng):
        k_ref, v_ref = qkv_refs[3 * g + 1], qkv_refs[3 * g + 2]
        rows = tile // dilations[g]
        carry_refs[2 * g][...] = k_ref[0, :, rows - blk:rows, :]
        carry_refs[2 * g + 1][...] = v_ref[0, :, rows - blk:rows, :]

    chunk = 256

    def finish(c, carry):
        r0 = pl.multiple_of(c * chunk, chunk)
        for hh in range(heads):
            ln = slice(hh * LANES, (hh + 1) * LANES)
            ls = [l_scr[hh, g, pl.ds(r0, chunk), :] for g in range(ng)]
            mx = functools.reduce(jnp.maximum, ls)
            es = [jnp.exp(l - mx) for l in ls]
            den = functools.reduce(lambda a, b: a + b, es)
            num = functools.reduce(
                lambda a, b: a + b, [es[g] * o_scr[hh, g, pl.ds(r0, chunk), :] for g in range(ng)])
            zz = z_ref[0, pl.ds(r0, chunk), ln].astype(F32)
            y = (num / den) * (zz / (1.0 + jnp.exp(-zz)))
            o_ref[0, pl.ds(r0, chunk), ln] = y.astype(o_ref.dtype)
        return carry

    lax.fori_loop(0, tile // chunk, finish, 0)


def _dsa_attention(hs, z, batch, seq):
    heads = DSA_HEADS_PER_STEP
    wl = heads * LANES
    ncol = DSA_WIDTH // wl
    tile = DSA_TILE
    dil = tuple(d for _, d in DSA_PATTERNS)
    spans = tuple(w // d for w, d in DSA_PATTERNS)
    for d, sp in zip(dil, spans):
        assert sp <= DSA_BLOCK and tile % (d * DSA_BLOCK) == 0 and seq % tile == 0

    in_specs, args, scratch = [], [], []
    for g, d in enumerate(dil):
        hv = hs[g].reshape(batch, d, seq // d, 3 * DSA_WIDTH)
        for kind in range(3):
            in_specs.append(pl.BlockSpec(
                (1, d, tile // d, wl), lambda b, h, t, kind=kind: (b, 0, t, kind * ncol + h)))
            args.append(hv)
        scratch += [pltpu.VMEM((d, DSA_BLOCK, wl), BF16)] * 2
    in_specs.append(pl.BlockSpec((1, tile, wl), lambda b, h, t: (b, t, h)))
    args.append(z.reshape(batch, seq, DSA_WIDTH))
    scratch += [pltpu.VMEM((heads, len(dil), tile, LANES), F32)] * 2

    out = pl.pallas_call(
        functools.partial(_dsa_attn_kernel, dilations=dil, spans=spans, heads=heads),
        grid=(batch, ncol, seq // tile),
        in_specs=in_specs,
        out_specs=pl.BlockSpec((1, tile, wl), lambda b, h, t: (b, t, h)),
        out_shape=jax.ShapeDtypeStruct((batch, seq, DSA_WIDTH), BF16),
        scratch_shapes=scratch,
        compiler_params=_params(("arbitrary",) * 3, 56),
        name="dsa_attention",
    )(*args)
    return out.reshape(batch * seq, DSA_WIDTH)


def _out_ln_kernel(y_ref, w_ref, x_ref, g_ref, b_ref, o_ref, ob_ref):
    r = DEEPNORM_ALPHA * x_ref[...] + jnp.dot(y_ref[...], w_ref[...], preferred_element_type=F32)
    mu = jnp.mean(r, axis=-1, keepdims=True)
    xc = r - mu
    var = jnp.mean(xc * xc, axis=-1, keepdims=True)
    out = xc * lax.rsqrt(var + LN_EPS) * g_ref[...] + b_ref[...]
    o_ref[...] = out
    ob_ref[...] = out.astype(BF16)


def _out_ln(y, w, x, g, b, tm=256):
    m, k = y.shape
    n = w.shape[1]
    row = lambda i: (i, 0)
    fixed = lambda i: (0, 0)
    return pl.pallas_call(
        _out_ln_kernel,
        grid=(m // tm,),
        in_specs=[pl.BlockSpec((tm, k), row), pl.BlockSpec((k, n), fixed), pl.BlockSpec((tm, n), row),
                  pl.BlockSpec((1, n), fixed), pl.BlockSpec((1, n), fixed)],
        out_specs=[pl.BlockSpec((tm, n), row), pl.BlockSpec((tm, n), row)],
        out_shape=[jax.ShapeDtypeStruct((m, n), F32), jax.ShapeDtypeStruct((m, n), BF16)],
        compiler_params=_params(("arbitrary",), 48),
        name="out_proj_layernorm",
    )(y, w, x, g, b)


def _mla_proj_kernel(x_ref, w1_ref, wqn_ref, wqp_ref, wkn_ref, wv_ref, qg_ref, kvg_ref,
                     cos_ref, sa_ref, sb_ref, q_ref, k_ref, v_ref, *, scale):
    h1 = jnp.dot(x_ref[...], w1_ref[...], preferred_element_type=F32)
    cq = h1[:, :MLA_Q_RANK]
    ckv = h1[:, MLA_Q_RANK:MLA_Q_RANK + MLA_KV_RANK]
    kpe = h1[:, MLA_Q_RANK + MLA_KV_RANK:]

    def rms(c, g):
        y = c * lax.rsqrt(jnp.mean(c * c, axis=-1, keepdims=True) + RMS_EPS)
        return (y * g).astype(BF16)

    cos, sa, sb = cos_ref[...], sa_ref[...], sb_ref[...]

    def rope(xs):
        return xs * cos + pltpu.roll(xs, MLA_ROPE // 2, 1) * sa + pltpu.roll(xs, LANES - MLA_ROPE // 2, 1) * sb

    cqn = rms(cq, qg_ref[...])
    ckvn = rms(ckv, kvg_ref[...])
    kpe_r = rope(kpe).astype(BF16)
    qn = jnp.dot(cqn, wqn_ref[...], preferred_element_type=F32)
    qp = jnp.dot(cqn, wqp_ref[...], preferred_element_type=F32)
    kn = jnp.dot(ckvn, wkn_ref[...], preferred_element_type=F32)
    v_ref[...] = jnp.dot(ckvn, wv_ref[...], preferred_element_type=F32).astype(BF16)
    for h in range(MLA_HEADS):
        hs = slice(h * LANES, (h + 1) * LANES)
        lo = slice(h * MLA_QK, h * MLA_QK + LANES)
        hi = slice(h * MLA_QK + LANES, (h + 1) * MLA_QK)
        q_ref[:, lo] = (qn[:, hs] * scale).astype(BF16)
        q_ref[:, hi] = (rope(qp[:, hs]) * scale).astype(BF16)
        k_ref[:, lo] = kn[:, hs].astype(BF16)
        k_ref[:, hi] = kpe_r


def _mla_proj(xb, w1, wqn, wqp, wkn, wv, qg, kvg, cos, sa, sb, scale, tm=256):
    m, k = xb.shape
    row = lambda i: (i, 0)
    fixed = lambda i: (0, 0)
    wspec = lambda w: pl.BlockSpec(w.shape, fixed)
    return pl.pallas_call(
        functools.partial(_mla_proj_kernel, scale=scale),
        grid=(m // tm,),
        in_specs=[pl.BlockSpec((tm, k), row), wspec(w1), wspec(wqn), wspec(wqp), wspec(wkn), wspec(wv),
                  wspec(qg), wspec(kvg)] + [pl.BlockSpec((tm, LANES), row)] * 3,
        out_specs=[pl.BlockSpec((tm, MLA_HEADS * MLA_QK), row), pl.BlockSpec((tm, MLA_HEADS * MLA_QK), row),
                   pl.BlockSpec((tm, MLA_WIDTH), row)],
        out_shape=[jax.ShapeDtypeStruct((m, MLA_HEADS * MLA_QK), BF16),
                   jax.ShapeDtypeStruct((m, MLA_HEADS * MLA_QK), BF16),
                   jax.ShapeDtypeStruct((m, MLA_WIDTH), BF16)],
        compiler_params=_params(("arbitrary",), 56),
        name="mla_projections",
    )(xb, w1, wqn, wqp, wkn, wv, qg, kvg, cos, sa, sb)


def _mla_flash_kernel(qi_ref, ki_ref, q_ref, k_ref, v_ref, z_ref, o_ref, m_scr, l_scr, acc_scr, *, heads, sub):
    t = pl.program_id(2)
    qi = qi_ref[t]
    ki = ki_ref[t]
    tq = q_ref.shape[1]
    tk = k_ref.shape[1]

    @pl.when(ki == 0)
    def _():
        m_scr[...] = jnp.full(m_scr.shape, -jnp.inf, F32)
        l_scr[...] = jnp.zeros(l_scr.shape, F32)
        acc_scr[...] = jnp.zeros(acc_scr.shape, F32)

    def update(masked):
        for hh in range(heads):
            k = k_ref[0, :, hh * MLA_QK:(hh + 1) * MLA_QK]
            v = v_ref[0, :, hh * MLA_V:(hh + 1) * MLA_V]
            for r0 in range(0, tq, sub):
                q = q_ref[0, r0:r0 + sub, hh * MLA_QK:(hh + 1) * MLA_QK]
                s = lax.dot_general(q, k, (((1,), (1,)), ((), ())), preferred_element_type=F32)
                if masked:
                    row = lax.broadcasted_iota(jnp.int32, (sub, tk), 0) + r0
                    col = lax.broadcasted_iota(jnp.int32, (sub, tk), 1)
                    s = jnp.where(row >= col, s, -jnp.inf)
                m_prev = m_scr[hh, r0:r0 + sub, :]
                l_prev = l_scr[hh, r0:r0 + sub, :]
                m_new = jnp.maximum(m_prev, jnp.max(s, axis=1, keepdims=True))
                alpha = jnp.exp(m_prev - m_new)
                p = jnp.exp(s - m_new[:, :1])
                l_scr[hh, r0:r0 + sub, :] = alpha * l_prev + jnp.sum(p, axis=1, keepdims=True)
                m_scr[hh, r0:r0 + sub, :] = m_new
                acc_scr[hh, r0:r0 + sub, :] = alpha * acc_scr[hh, r0:r0 + sub, :] + jnp.dot(
                    p.astype(BF16), v, preferred_element_type=F32)

    @pl.when(ki < qi)
    def _():
        update(False)

    @pl.when(ki == qi)
    def _():
        update(True)
        for hh in range(heads):
            ln = slice(hh * MLA_V, (hh + 1) * MLA_V)
            zz = z_ref[0, :, ln].astype(F32)
            o = acc_scr[hh] * (1.0 / l_scr[hh])
            o_ref[0, :, ln] = (o * (zz / (1.0 + jnp.exp(-zz)))).astype(o_ref.dtype)


def _mla_flash(q, k, v, z, batch, seq):
    heads = MLA_HEADS_PER_STEP
    tile = MLA_TILE
    nt = seq // tile
    qi = np.concatenate([np.full(i + 1, i, np.int32) for i in range(nt)])
    ki = np.concatenate([np.arange(i + 1, dtype=np.int32) for i in range(nt)])
    qk_w = heads * MLA_QK
    v_w = heads * MLA_V
    qmap = lambda b, h, t, qi_r, ki_r: (b, qi_r[t], h)
    kmap = lambda b, h, t, qi_r, ki_r: (b, ki_r[t], h)
    grid_spec = pltpu.PrefetchScalarGridSpec(
        num_scalar_prefetch=2,
        grid=(batch, MLA_HEADS // heads, len(qi)),
        in_specs=[pl.BlockSpec((1, tile, qk_w), qmap), pl.BlockSpec((1, tile, qk_w), kmap),
                  pl.BlockSpec((1, tile, v_w), kmap), pl.BlockSpec((1, tile, v_w), qmap)],
        out_specs=pl.BlockSpec((1, tile, v_w), qmap),
        scratch_shapes=[pltpu.VMEM((heads, tile, LANES), F32)] * 3,
    )
    out = pl.pallas_call(
        functools.partial(_mla_flash_kernel, heads=heads, sub=512),
        grid_spec=grid_spec,
        out_shape=jax.ShapeDtypeStruct((batch, seq, MLA_WIDTH), BF16),
        compiler_params=_params(("arbitrary",) * 3, 56),
        name="mla_flash_attention",
    )(jnp.asarray(qi), jnp.asarray(ki),
      q.reshape(batch, seq, MLA_HEADS * MLA_QK), k.reshape(batch, seq, MLA_HEADS * MLA_QK),
      v.reshape(batch, seq, MLA_WIDTH), z.reshape(batch, seq, MLA_WIDTH))
    return out.reshape(batch * seq, MLA_WIDTH)


def _inv_freq(dim):
    return 1.0 / (ROPE_THETA ** (jnp.arange(0, dim, 2, dtype=F32) / dim))


def kernel(x, positions, dsa_w_in, dsa_w_out, mla_w_in, mla_q_norm, mla_w_uq, mla_kv_norm, mla_w_ukv,
           mla_w_out, ln_g, ln_b):
    batch, seq, dm = x.shape
    tokens = batch * seq
    half = DSA_HEAD_DIM // 2

    inv_a = _inv_freq(DSA_HEAD_DIM)
    inv_row_a = jnp.concatenate([inv_a, inv_a])[None, :]
    mult_a = jnp.stack([jnp.ones((LANES,), F32),
                        jnp.concatenate([-jnp.ones((half,), F32), jnp.ones((half,), F32)])])
    tables_a = []
    for _, d in DSA_PATTERNS:
        pos_d = positions.reshape(batch, seq // d, d).transpose(0, 2, 1).reshape(tokens, 1).astype(F32)
        tables_a.append(_rope_tables(pos_d, inv_row_a, mult_a, ("cos", "sin")))

    inv_b = _inv_freq(MLA_ROPE)
    hb = MLA_ROPE // 2
    zeros_h = jnp.zeros((hb,), F32)
    ones_h = jnp.ones((hb,), F32)
    pad = jnp.zeros((LANES - MLA_ROPE,), F32)
    inv_row_b = jnp.concatenate([inv_b, inv_b, pad])[None, :]
    mult_b = jnp.stack([jnp.concatenate([ones_h, ones_h, pad]),
                        jnp.concatenate([zeros_h, ones_h, pad]),
                        jnp.concatenate([-ones_h, zeros_h, pad])])
    cos_b, sa_b, sb_b = _rope_tables(positions.reshape(tokens, 1).astype(F32), inv_row_b, mult_b,
                                     ("cos", "sin", "sin"))

    x2 = x.reshape(tokens, dm)
    xb = None
    qkv_w = 3 * DSA_WIDTH
    for layer in range(DEPTH):
        j = layer // N_MIXERS
        if layer % N_MIXERS == 0:
            w_in = dsa_w_in[j].astype(BF16)
            x3 = x2.reshape(batch, seq, dm)
            hs = []
            xb_nat = None
            for g, (_, d) in enumerate(DSA_PATTERNS):
                xb_d = xb if (d == 1 and xb is not None) else _cast_perm(x3, d)
                if d == 1:
                    xb_nat = xb_d
                cos, sin = tables_a[g]
                hs.append(_mm_rope(xb_d, w_in[:, g * qkv_w:(g + 1) * qkv_w], cos, sin, DSA_HEAD_DIM ** -0.5))
            if xb_nat is None:
                xb_nat = xb if xb is not None else _cast_perm(x3, 1)
            z = _mm(xb_nat, w_in[:, DSA_GROUPS * qkv_w:])
            y = _dsa_attention(hs, z, batch, seq)
            w_out = dsa_w_out[j].astype(BF16)
        else:
            if xb is None:
                xb = _cast_perm(x2.reshape(batch, seq, dm), 1)
            w_in = mla_w_in[j]
            o2 = MLA_Q_RANK + MLA_KV_RANK
            o3 = o2 + MLA_ROPE
            w1 = jnp.pad(w_in[:, :o3], ((0, 0), (0, LANES - MLA_ROPE))).astype(BF16)
            wz = w_in[:, o3:].astype(BF16)
            wq = mla_w_uq[j].reshape(MLA_Q_RANK, MLA_HEADS, MLA_NOPE + MLA_ROPE)
            wqn = wq[:, :, :MLA_NOPE].reshape(MLA_Q_RANK, MLA_HEADS * MLA_NOPE).astype(BF16)
            wqp = jnp.pad(wq[:, :, MLA_NOPE:], ((0, 0), (0, 0), (0, LANES - MLA_ROPE))).reshape(
                MLA_Q_RANK, MLA_HEADS * LANES).astype(BF16)
            wkv = mla_w_ukv[j].reshape(MLA_KV_RANK, MLA_HEADS, MLA_NOPE + MLA_V)
            wkn = wkv[:, :, :MLA_NOPE].reshape(MLA_KV_RANK, MLA_HEADS * MLA_NOPE).astype(BF16)
            wv = wkv[:, :, MLA_NOPE:].reshape(MLA_KV_RANK, MLA_HEADS * MLA_V).astype(BF16)
            q, k, v = _mla_proj(xb, w1, wqn, wqp, wkn, wv, mla_q_norm[j][None, :], mla_kv_norm[j][None, :],
                                cos_b, sa_b, sb_b, (MLA_NOPE + MLA_ROPE) ** -0.5)
            z = _mm(xb, wz)
            y = _mla_flash(q, k, v, z, batch, seq)
            w_out = mla_w_out[j].astype(BF16)
        x2, xb = _out_ln(y, w_out, x2, ln_g[layer][None, :], ln_b[layer][None, :])
    return x2.reshape(batch, seq, dm)
```

```python
import functools
import math

import jax
import jax.numpy as jnp
import numpy as np
from jax import lax
from jax.experimental import pallas as pl
from jax.experimental.pallas import tpu as pltpu

F32 = jnp.float32
BF16 = jnp.bfloat16

D_MODEL = 2048
DEPTH = 4
N_MIXERS = 2

DSA_PATTERNS = ((128, 1), (512, 4), (2048, 16))
DSA_GROUPS = len(DSA_PATTERNS)
DSA_HEADS = 16
DSA_HEAD_DIM = 128
DSA_WIDTH = DSA_HEADS * DSA_HEAD_DIM
DSA_BLOCK = 128

MLA_HEADS = 16
MLA_Q_RANK = 512
MLA_KV_RANK = 512
MLA_NOPE = 128
MLA_ROPE = 64
MLA_V = 128
MLA_WIDTH = MLA_HEADS * MLA_V

ROPE_THETA = 10000.0
RMS_EPS = 1e-6
LN_EPS = 1e-5
DEEPNORM_ALPHA = (2 * DEPTH) ** 0.25

LANES = 128
MIB = 1024 * 1024

DSA_TILE = 2048
DSA_HEADS_PER_STEP = 2
DSA_BLOCKS_PER_ITER = 4
MLA_TILE = 1024
MLA_HEADS_PER_STEP = 4
MLA_QK = 2 * LANES


def _params(semantics, vmem_mib):
    return pltpu.CompilerParams(dimension_semantics=semantics, vmem_limit_bytes=vmem_mib * MIB)


def _rope_table_kernel(pos_ref, inv_ref, mult_ref, *out_refs, kinds):
    ang = pos_ref[...] * inv_ref[...]
    c = jnp.cos(ang)
    s = jnp.sin(ang)
    for i, (kind, o_ref) in enumerate(zip(kinds, out_refs)):
        o_ref[...] = (c if kind == "cos" else s) * mult_ref[i:i + 1, :]


def _rope_tables(pos_f32, inv_row, mult_rows, kinds, tm=2048):
    n = pos_f32.shape[0]
    nk = len(kinds)
    return pl.pallas_call(
        functools.partial(_rope_table_kernel, kinds=kinds),
        grid=(n // tm,),
        in_specs=[pl.BlockSpec((tm, 1), lambda i: (i, 0)),
                  pl.BlockSpec((1, LANES), lambda i: (0, 0)),
                  pl.BlockSpec((nk, LANES), lambda i: (0, 0))],
        out_specs=[pl.BlockSpec((tm, LANES), lambda i: (i, 0))] * nk,
        out_shape=[jax.ShapeDtypeStruct((n, LANES), F32)] * nk,
        compiler_params=_params(("arbitrary",), 32),
        name="rope_tables",
    )(pos_f32, inv_row, mult_rows)


def _cast_perm_kernel(x_ref, *refs, dilations):
    o_refs, col_scr = refs[:-1], refs[-1]
    tm, dm = x_ref.shape[1], x_ref.shape[2]
    strided = any(d != 1 for d in dilations)
    for c in range(dm // LANES):
        cs = slice(c * LANES, (c + 1) * LANES)
        if strided:
            col_scr[c] = x_ref[0, :, cs]
        for d, o_ref in zip(dilations, o_refs):
            if d == 1:
                o_ref[0, 0, :, cs] = x_ref[0, :, cs].astype(BF16)
            else:
                for r in range(d):
                    o_ref[0, r, :, cs] = col_scr[c, pl.ds(r, tm // d, stride=d), :].astype(BF16)


def _cast_perm(x3, dilations, tm=1024):
    b, s, dm = x3.shape
    outs = pl.pallas_call(
        functools.partial(_cast_perm_kernel, dilations=dilations),
        grid=(b, s // tm),
        in_specs=[pl.BlockSpec((1, tm, dm), lambda bi, t: (bi, t, 0))],
        out_specs=[pl.BlockSpec((1, d, tm // d, dm), lambda bi, t: (bi, 0, t, 0)) for d in dilations],
        out_shape=[jax.ShapeDtypeStruct((b, d, s // d, dm), BF16) for d in dilations],
        scratch_shapes=[pltpu.VMEM((dm // LANES, tm, LANES), F32)],
        compiler_params=_params(("arbitrary",) * 2, 48),
        name="cast_perm",
    )(x3)
    return [o.reshape(b * s, dm) for o in outs]


def _mm_kernel(a_ref, w_ref, o_ref):
    o_ref[...] = jnp.dot(a_ref[...], w_ref[...], preferred_element_type=F32).astype(o_ref.dtype)


def _mm(a, w, col0=0, ncols=None, out_dtype=BF16, tm=1024, tn=1024):
    m, k = a.shape
    n = w.shape[1] - col0 if ncols is None else ncols
    j0 = col0 // tn
    return pl.pallas_call(
        _mm_kernel,
        grid=(m // tm, n // tn),
        in_specs=[pl.BlockSpec((tm, k), lambda i, j: (i, 0)),
                  pl.BlockSpec((k, tn), lambda i, j: (0, j + j0))],
        out_specs=pl.BlockSpec((tm, tn), lambda i, j: (i, j)),
        out_shape=jax.ShapeDtypeStruct((m, n), out_dtype),
        compiler_params=_params(("arbitrary", "arbitrary"), 48),
        name="matmul",
    )(a, w)


def _mm_rope_kernel(a_ref, w_ref, cos_ref, sin_ref, o_ref, *, n_q_blocks, n_rope_blocks, q_scale):
    acc = jnp.dot(a_ref[...], w_ref[...], preferred_element_type=F32)
    j = pl.program_id(1)

    @pl.when(j < n_rope_blocks)
    def _():
        scale = jnp.where(j < n_q_blocks, q_scale, 1.0).astype(F32)
        cos = cos_ref[...]
        sin = sin_ref[...] * scale
        cos = cos * scale
        for c in range(acc.shape[1] // LANES):
            xs = acc[:, c * LANES:(c + 1) * LANES]
            o_ref[:, c * LANES:(c + 1) * LANES] = (
                xs * cos + pltpu.roll(xs, LANES // 2, 1) * sin).astype(o_ref.dtype)

    @pl.when(j >= n_rope_blocks)
    def _():
        o_ref[...] = acc.astype(o_ref.dtype)


def _mm_rope(a, w, col0, cos, sin, q_scale, tm=1024, tn=1024):
    m, k = a.shape
    n = 3 * DSA_WIDTH
    j0 = col0 // tn
    kern = functools.partial(_mm_rope_kernel, n_q_blocks=DSA_WIDTH // tn,
                             n_rope_blocks=2 * DSA_WIDTH // tn, q_scale=q_scale)
    return pl.pallas_call(
        kern,
        grid=(m // tm, n // tn),
        in_specs=[pl.BlockSpec((tm, k), lambda i, j: (i, 0)),
                  pl.BlockSpec((k, tn), lambda i, j: (0, j + j0)),
                  pl.BlockSpec((tm, LANES), lambda i, j: (i, 0)),
                  pl.BlockSpec((tm, LANES), lambda i, j: (i, 0))],
        out_specs=pl.BlockSpec((tm, tn), lambda i, j: (i, j)),
        out_shape=jax.ShapeDtypeStruct((m, n), BF16),
        compiler_params=_params(("arbitrary", "arbitrary"), 48),
        name="matmul_rope",
    )(a, w, cos, sin)


def _band_attention(q, kwin, vwin, bias):
    s = lax.dot_general(q, kwin, (((1,), (1,)), ((), ())), preferred_element_type=F32) + bias
    m = jnp.max(s, axis=1, keepdims=True)
    p = jnp.exp(s - m)
    l = jnp.sum(p, axis=1, keepdims=True)
    o = jnp.dot(p.astype(BF16), vwin, preferred_element_type=F32) * (1.0 / l)
    lse = m + jnp.log(l)
    return o, jnp.broadcast_to(lse, o.shape)


def _dsa_attn_kernel(*refs, dilations, spans, heads, unroll):
    ng = len(dilations)
    qkv_refs = refs[:3 * ng]
    z_ref, o_ref = refs[3 * ng], refs[3 * ng + 1]
    win_refs = refs[3 * ng + 2:3 * ng + 2 + 2 * ng]
    bias_ref, o_scr, l_scr = refs[3 * ng + 2 + 2 * ng:]
    blk = DSA_BLOCK
    tile = z_ref.shape[1]
    t = pl.program_id(2)

    for g in range(ng):
        rows = tile // dilations[g]
        for src, dst in ((qkv_refs[3 * g + 1], win_refs[2 * g]), (qkv_refs[3 * g + 2], win_refs[2 * g + 1])):
            @pl.when(t == 0)
            def _(dst=dst):
                dst[:, 0:blk, :] = jnp.zeros((dst.shape[0], blk, dst.shape[2]), dst.dtype)

            @pl.when(t > 0)
            def _(dst=dst, rows=rows):
                dst[:, 0:blk, :] = dst[:, rows:rows + blk, :]

            dst[:, blk:blk + rows, :] = src[0]

    row = lax.broadcasted_iota(jnp.int32, (blk, 2 * blk), 0)
    col = lax.broadcasted_iota(jnp.int32, (blk, 2 * blk), 1)
    dist = row + blk - col
    for g in range(ng):
        band = (dist >= 0) & (dist <= spans[g])
        bias_ref[g, 0] = jnp.where(band, 0.0, -jnp.inf).astype(F32)
        bias_ref[g, 1] = jnp.where(band & (col >= blk), 0.0, -jnp.inf).astype(F32)

    n_items = tile // blk
    for g in range(ng):
        d = dilations[g]
        q_ref = qkv_refs[3 * g]
        kw_ref, vw_ref = win_refs[2 * g:2 * g + 2]
        nb = n_items // d

        def items(it, carry, g=g, d=d, q_ref=q_ref, kw_ref=kw_ref, vw_ref=vw_ref, nb=nb):
            for u in range(unroll):
                idx = it * unroll + u
                r = idx // nb
                bb = idx % nb
                r0 = pl.multiple_of(bb * blk, blk)
                first = jnp.logical_and(t == 0, bb == 0).astype(jnp.int32)
                bias = bias_ref[g, first]
                start = r0 * d + r
                rows_idx = pl.ds(start, blk) if d == 1 else pl.ds(start, blk, stride=d)
                for hh in range(heads):
                    ln = slice(hh * LANES, (hh + 1) * LANES)
                    o, lse = _band_attention(q_ref[0, r, pl.ds(r0, blk), ln],
                                             kw_ref[r, pl.ds(r0, 2 * blk), ln],
                                             vw_ref[r, pl.ds(r0, 2 * blk), ln], bias)
                    o_scr[hh, g, rows_idx, :] = o
                    l_scr[hh, g, rows_idx, :] = lse
            return carry

        lax.fori_loop(0, n_items // unroll, items, 0)

    chunk = 256

    def finish(c, carry):
        r0 = pl.multiple_of(c * chunk, chunk)
        for hh in range(heads):
            ln = slice(hh * LANES, (hh + 1) * LANES)
            ls = [l_scr[hh, g, pl.ds(r0, chunk), :] for g in range(ng)]
            mx = functools.reduce(jnp.maximum, ls)
            es = [jnp.exp(l - mx) for l in ls]
            den = functools.reduce(lambda a, b: a + b, es)
            num = functools.reduce(
                lambda a, b: a + b, [es[g] * o_scr[hh, g, pl.ds(r0, chunk), :] for g in range(ng)])
            zz = z_ref[0, pl.ds(r0, chunk), ln].astype(F32)
            y = (num / den) * (zz / (1.0 + jnp.exp(-zz)))
            o_ref[0, pl.ds(r0, chunk), ln] = y.astype(o_ref.dtype)
        return carry

    lax.fori_loop(0, tile // chunk, finish, 0)


def _dsa_attention(hs, z, batch, seq):
    heads = DSA_HEADS_PER_STEP
    wl = heads * LANES
    ncol = DSA_WIDTH // wl
    tile = DSA_TILE
    dil = tuple(d for _, d in DSA_PATTERNS)
    spans = tuple(w // d for w, d in DSA_PATTERNS)
    for d, sp in zip(dil, spans):
        assert sp <= DSA_BLOCK and tile % (d * DSA_BLOCK) == 0 and seq % tile == 0
    assert (tile // DSA_BLOCK) % DSA_BLOCKS_PER_ITER == 0

    in_specs, args, scratch = [], [], []
    for g, d in enumerate(dil):
        hv = hs[g].reshape(batch, d, seq // d, 3 * DSA_WIDTH)
        for kind in range(3):
            in_specs.append(pl.BlockSpec(
                (1, d, tile // d, wl), lambda b, h, t, kind=kind: (b, 0, t, kind * ncol + h)))
            args.append(hv)
        scratch += [pltpu.VMEM((d, DSA_BLOCK + tile // d, wl), BF16)] * 2
    in_specs.append(pl.BlockSpec((1, tile, wl), lambda b, h, t: (b, t, h)))
    args.append(z.reshape(batch, seq, DSA_WIDTH))
    scratch.append(pltpu.VMEM((len(dil), 2, DSA_BLOCK, 2 * DSA_BLOCK), F32))
    scratch += [pltpu.VMEM((heads, len(dil), tile, LANES), F32)] * 2

    out = pl.pallas_call(
        functools.partial(_dsa_attn_kernel, dilations=dil, spans=spans, heads=heads,
                          unroll=DSA_BLOCKS_PER_ITER),
        grid=(batch, ncol, seq // tile),
        in_specs=in_specs,
        out_specs=pl.BlockSpec((1, tile, wl), lambda b, h, t: (b, t, h)),
        out_shape=jax.ShapeDtypeStruct((batch, seq, DSA_WIDTH), BF16),
        scratch_shapes=scratch,
        compiler_params=_params(("arbitrary",) * 3, 56),
        name="dsa_attention",
    )(*args)
    return out.reshape(batch * seq, DSA_WIDTH)


def _out_ln_kernel(y_ref, w_ref, x_ref, g_ref, b_ref, o_ref, ob_ref):
    r = DEEPNORM_ALPHA * x_ref[...] + jnp.dot(y_ref[...], w_ref[...], preferred_element_type=F32)
    mu = jnp.mean(r, axis=-1, keepdims=True)
    xc = r - mu
    var = jnp.mean(xc * xc, axis=-1, keepdims=True)
    out = xc * lax.rsqrt(var + LN_EPS) * g_ref[...] + b_ref[...]
    o_ref[...] = out
    ob_ref[...] = out.astype(BF16)


def _out_ln(y, w, x, g, b, tm=256):
    m, k = y.shape
    n = w.shape[1]
    row = lambda i: (i, 0)
    fixed = lambda i: (0, 0)
    return pl.pallas_call(
        _out_ln_kernel,
        grid=(m // tm,),
        in_specs=[pl.BlockSpec((tm, k), row), pl.BlockSpec((k, n), fixed), pl.BlockSpec((tm, n), row),
                  pl.BlockSpec((1, n), fixed), pl.BlockSpec((1, n), fixed)],
        out_specs=[pl.BlockSpec((tm, n), row), pl.BlockSpec((tm, n), row)],
        out_shape=[jax.ShapeDtypeStruct((m, n), F32), jax.ShapeDtypeStruct((m, n), BF16)],
        compiler_params=_params(("arbitrary",), 48),
        name="out_proj_layernorm",
    )(y, w, x, g, b)


def _mla_proj_kernel(x_ref, w1_ref, wqn_ref, wqp_ref, wkn_ref, wvt_ref, qg_ref, kvg_ref,
                     cos_ref, sa_ref, sb_ref, q_ref, k_ref, vt_ref, *, scale):
    h1 = jnp.dot(x_ref[0], w1_ref[...], preferred_element_type=F32)
    cq = h1[:, :MLA_Q_RANK]
    ckv = h1[:, MLA_Q_RANK:MLA_Q_RANK + MLA_KV_RANK]
    kpe = h1[:, MLA_Q_RANK + MLA_KV_RANK:]

    def rms(c, g):
        y = c * lax.rsqrt(jnp.mean(c * c, axis=-1, keepdims=True) + RMS_EPS)
        return (y * g).astype(BF16)

    cos, sa, sb = cos_ref[0], sa_ref[0], sb_ref[0]

    def rope(xs):
        return xs * cos + pltpu.roll(xs, MLA_ROPE // 2, 1) * sa + pltpu.roll(xs, LANES - MLA_ROPE // 2, 1) * sb

    cqn = rms(cq, qg_ref[...])
    ckvn = rms(ckv, kvg_ref[...])
    kpe_r = rope(kpe).astype(BF16)
    qn = jnp.dot(cqn, wqn_ref[...], preferred_element_type=F32)
    qp = jnp.dot(cqn, wqp_ref[...], preferred_element_type=F32)
    kn = jnp.dot(ckvn, wkn_ref[...], preferred_element_type=F32)
    vt = lax.dot_general(wvt_ref[...], ckvn, (((1,), (1,)), ((), ())), preferred_element_type=F32)
    for h in range(MLA_HEADS):
        hs = slice(h * LANES, (h + 1) * LANES)
        q_ref[0, h, :, 0:LANES] = (qn[:, hs] * scale).astype(BF16)
        q_ref[0, h, :, LANES:MLA_QK] = (rope(qp[:, hs]) * scale).astype(BF16)
        k_ref[0, h, :, 0:LANES] = kn[:, hs].astype(BF16)
        k_ref[0, h, :, LANES:MLA_QK] = kpe_r
        vt_ref[0, h] = vt[h * MLA_V:(h + 1) * MLA_V, :].astype(BF16)


def _mla_proj(xb, w1, wqn, wqp, wkn, wvt, qg, kvg, cos, sa, sb, scale, batch, seq, tm=256):
    dm = xb.shape[1]
    row = lambda b, i: (b, i, 0)
    fixed = lambda b, i: (0, 0)
    wspec = lambda w: pl.BlockSpec(w.shape, fixed)
    tab = lambda a: a.reshape(batch, seq, LANES)
    return pl.pallas_call(
        functools.partial(_mla_proj_kernel, scale=scale),
        grid=(batch, seq // tm),
        in_specs=[pl.BlockSpec((1, tm, dm), row), wspec(w1), wspec(wqn), wspec(wqp), wspec(wkn), wspec(wvt),
                  wspec(qg), wspec(kvg)] + [pl.BlockSpec((1, tm, LANES), row)] * 3,
        out_specs=[pl.BlockSpec((1, MLA_HEADS, tm, MLA_QK), lambda b, i: (b, 0, i, 0)),
                   pl.BlockSpec((1, MLA_HEADS, tm, MLA_QK), lambda b, i: (b, 0, i, 0)),
                   pl.BlockSpec((1, MLA_HEADS, MLA_V, tm), lambda b, i: (b, 0, 0, i))],
        out_shape=[jax.ShapeDtypeStruct((batch, MLA_HEADS, seq, MLA_QK), BF16),
                   jax.ShapeDtypeStruct((batch, MLA_HEADS, seq, MLA_QK), BF16),
                   jax.ShapeDtypeStruct((batch, MLA_HEADS, MLA_V, seq), BF16)],
        compiler_params=_params(("arbitrary",) * 2, 56),
        name="mla_projections",
    )(xb.reshape(batch, seq, dm), w1, wqn, wqp, wkn, wvt, qg, kvg, tab(cos), tab(sa), tab(sb))


def _mla_flash_kernel(qi_ref, ki_ref, q_ref, k_ref, vt_ref, z_ref, o_ref, s_buf, m_scr, l_scr, acc_scr, *, heads):
    t = pl.program_id(2)
    qi = qi_ref[t]
    ki = ki_ref[t]
    tq = q_ref.shape[2]
    tk = k_ref.shape[2]

    @pl.when(ki == 0)
    def _():
        m_scr[...] = jnp.full(m_scr.shape, -jnp.inf, F32)
        l_scr[...] = jnp.zeros(l_scr.shape, F32)
        acc_scr[...] = jnp.zeros(acc_scr.shape, F32)

    def scores(h):
        return lax.dot_general(k_ref[0, h], q_ref[0, h], (((1,), (1,)), ((), ())), preferred_element_type=F32)

    def softmax_pv(h, s, masked):
        if masked:
            key = lax.broadcasted_iota(jnp.int32, (tk, tq), 0)
            qry = lax.broadcasted_iota(jnp.int32, (tk, tq), 1)
            s = jnp.where(qry >= key, s, -jnp.inf)
        m_prev = m_scr[h]
        m_new = jnp.maximum(m_prev, jnp.max(s, axis=0, keepdims=True))
        alpha = jnp.exp2(m_prev - m_new)
        p = jnp.exp2(s - m_new)
        l_scr[h] = alpha * l_scr[h] + jnp.sum(p, axis=0, keepdims=True)
        m_scr[h] = m_new
        acc_scr[h] = alpha * acc_scr[h] + jnp.dot(vt_ref[0, h], p.astype(BF16), preferred_element_type=F32)

    def update(masked):
        s_buf[0] = scores(0)
        for h in range(heads):
            if h + 1 < heads:
                s_buf[(h + 1) % 2] = scores(h + 1)
            softmax_pv(h, s_buf[h % 2], masked)

    @pl.when(ki < qi)
    def _():
        update(False)

    @pl.when(ki == qi)
    def _():
        update(True)
        for h in range(heads):
            ln = slice(h * MLA_V, (h + 1) * MLA_V)
            zz = z_ref[0, :, ln].astype(F32)
            o = (acc_scr[h] * (1.0 / l_scr[h])).T
            o_ref[0, :, ln] = (o * (zz / (1.0 + jnp.exp(-zz)))).astype(o_ref.dtype)


def _mla_flash(q, k, vt, z, batch, seq):
    heads = MLA_HEADS_PER_STEP
    tile = MLA_TILE
    nt = seq // tile
    qi = np.concatenate([np.full(i + 1, i, np.int32) for i in range(nt)])
    ki = np.concatenate([np.arange(i + 1, dtype=np.int32) for i in range(nt)])
    v_w = heads * MLA_V
    qmap = lambda b, h, t, qi_r, ki_r: (b, h, qi_r[t], 0)
    kmap = lambda b, h, t, qi_r, ki_r: (b, h, ki_r[t], 0)
    vmap = lambda b, h, t, qi_r, ki_r: (b, h, 0, ki_r[t])
    zmap = lambda b, h, t, qi_r, ki_r: (b, qi_r[t], h)
    grid_spec = pltpu.PrefetchScalarGridSpec(
        num_scalar_prefetch=2,
        grid=(batch, MLA_HEADS // heads, len(qi)),
        in_specs=[pl.BlockSpec((1, heads, tile, MLA_QK), qmap), pl.BlockSpec((1, heads, tile, MLA_QK), kmap),
                  pl.BlockSpec((1, heads, MLA_V, tile), vmap), pl.BlockSpec((1, tile, v_w), zmap)],
        out_specs=pl.BlockSpec((1, tile, v_w), zmap),
        scratch_shapes=[pltpu.VMEM((2, tile, tile), F32), pltpu.VMEM((heads, 1, tile), F32),
                        pltpu.VMEM((heads, 1, tile), F32), pltpu.VMEM((heads, MLA_V, tile), F32)],
    )
    out = pl.pallas_call(
        functools.partial(_mla_flash_kernel, heads=heads),
        grid_spec=grid_spec,
        out_shape=jax.ShapeDtypeStruct((batch, seq, MLA_WIDTH), BF16),
        compiler_params=_params(("arbitrary",) * 3, 56),
        name="mla_flash_attention",
    )(jnp.asarray(qi), jnp.asarray(ki), q, k, vt, z.reshape(batch, seq, MLA_WIDTH))
    return out.reshape(batch * seq, MLA_WIDTH)


def _inv_freq(dim):
    return 1.0 / (ROPE_THETA ** (jnp.arange(0, dim, 2, dtype=F32) / dim))


def kernel(x, positions, dsa_w_in, dsa_w_out, mla_w_in, mla_q_norm, mla_w_uq, mla_kv_norm, mla_w_ukv,
           mla_w_out, ln_g, ln_b):
    batch, seq, dm = x.shape
    tokens = batch * seq
    half = DSA_HEAD_DIM // 2
    dils = tuple(d for _, d in DSA_PATTERNS)

    inv_a = _inv_freq(DSA_HEAD_DIM)
    inv_row_a = jnp.concatenate([inv_a, inv_a])[None, :]
    mult_a = jnp.stack([jnp.ones((LANES,), F32),
                        jnp.concatenate([-jnp.ones((half,), F32), jnp.ones((half,), F32)])])
    tables_a = []
    for d in dils:
        pos_d = positions.reshape(batch, seq // d, d).transpose(0, 2, 1).reshape(tokens, 1).astype(F32)
        tables_a.append(_rope_tables(pos_d, inv_row_a, mult_a, ("cos", "sin")))

    inv_b = _inv_freq(MLA_ROPE)
    hb = MLA_ROPE // 2
    zeros_h = jnp.zeros((hb,), F32)
    ones_h = jnp.ones((hb,), F32)
    pad = jnp.zeros((LANES - MLA_ROPE,), F32)
    inv_row_b = jnp.concatenate([inv_b, inv_b, pad])[None, :]
    mult_b = jnp.stack([jnp.concatenate([ones_h, ones_h, pad]),
                        jnp.concatenate([zeros_h, ones_h, pad]),
                        jnp.concatenate([-ones_h, zeros_h, pad])])
    cos_b, sa_b, sb_b = _rope_tables(positions.reshape(tokens, 1).astype(F32), inv_row_b, mult_b,
                                     ("cos", "sin", "sin"))

    x2 = x.reshape(tokens, dm)
    xb = None
    qkv_w = 3 * DSA_WIDTH
    for layer in range(DEPTH):
        j = layer // N_MIXERS
        x3 = x2.reshape(batch, seq, dm)
        if layer % N_MIXERS == 0:
            w_in = dsa_w_in[j].astype(BF16)
            if xb is None:
                xbs = dict(zip(dils, _cast_perm(x3, dils)))
            else:
                rest = tuple(d for d in dils if d != 1)
                xbs = dict(zip(rest, _cast_perm(x3, rest)))
                xbs[1] = xb
            hs = []
            for g, d in enumerate(dils):
                cos, sin = tables_a[g]
                hs.append(_mm_rope(xbs[d], w_in, g * qkv_w, cos, sin, DSA_HEAD_DIM ** -0.5))
            z = _mm(xbs[1], w_in, DSA_GROUPS * qkv_w)
            y = _dsa_attention(hs, z, batch, seq)
            w_out = dsa_w_out[j].astype(BF16)
        else:
            if xb is None:
                xb = _cast_perm(x3, (1,))[0]
            w_in = mla_w_in[j]
            o2 = MLA_Q_RANK + MLA_KV_RANK
            o3 = o2 + MLA_ROPE
            w1 = jnp.pad(w_in[:, :o3], ((0, 0), (0, LANES - MLA_ROPE))).astype(BF16)
            wz = w_in[:, o3:].astype(BF16)
            wq = mla_w_uq[j].reshape(MLA_Q_RANK, MLA_HEADS, MLA_NOPE + MLA_ROPE)
            wqn = wq[:, :, :MLA_NOPE].reshape(MLA_Q_RANK, MLA_HEADS * MLA_NOPE).astype(BF16)
            wqp = jnp.pad(wq[:, :, MLA_NOPE:], ((0, 0), (0, 0), (0, LANES - MLA_ROPE))).reshape(
                MLA_Q_RANK, MLA_HEADS * LANES).astype(BF16)
            wkv = mla_w_ukv[j].reshape(MLA_KV_RANK, MLA_HEADS, MLA_NOPE + MLA_V)
            wkn = wkv[:, :, :MLA_NOPE].reshape(MLA_KV_RANK, MLA_HEADS * MLA_NOPE).astype(BF16)
            wvt = wkv[:, :, MLA_NOPE:].reshape(MLA_KV_RANK, MLA_HEADS * MLA_V).T.astype(BF16)
            q_scale = (MLA_NOPE + MLA_ROPE) ** -0.5 * math.log2(math.e)
            q, k, vt = _mla_proj(xb, w1, wqn, wqp, wkn, wvt, mla_q_norm[j][None, :], mla_kv_norm[j][None, :],
                                 cos_b, sa_b, sb_b, q_scale, batch, seq)
            z = _mm(xb, wz)
            y = _mla_flash(q, k, vt, z, batch, seq)
            w_out = mla_w_out[j].astype(BF16)
        x2, xb = _out_ln(y, w_out, x2, ln_g[layer][None, :], ln_b[layer][None, :])
    return x2.reshape(batch, seq, dm)
```

```python
import functools
import math

import jax
import jax.numpy as jnp
import numpy as np
from jax import lax
from jax.experimental import pallas as pl
from jax.experimental.pallas import tpu as pltpu

F32 = jnp.float32
BF16 = jnp.bfloat16

D_MODEL = 2048
DEPTH = 4
N_MIXERS = 2

DSA_PATTERNS = ((128, 1), (512, 4), (2048, 16))
DSA_GROUPS = len(DSA_PATTERNS)
DSA_HEADS = 16
DSA_HEAD_DIM = 128
DSA_WIDTH = DSA_HEADS * DSA_HEAD_DIM
DSA_BLOCK = 128

MLA_HEADS = 16
MLA_Q_RANK = 512
MLA_KV_RANK = 512
MLA_NOPE = 128
MLA_ROPE = 64
MLA_V = 128
MLA_WIDTH = MLA_HEADS * MLA_V

ROPE_THETA = 10000.0
RMS_EPS = 1e-6
LN_EPS = 1e-5
DEEPNORM_ALPHA = (2 * DEPTH) ** 0.25

LANES = 128
MIB = 1024 * 1024

DSA_TILE = 2048
DSA_HEADS_PER_STEP = 2
DSA_BLOCKS_PER_ITER = 8
MLA_TILE = 1024
MLA_HEADS_PER_STEP = 8
MLA_VT_ROWS = MLA_V + 16
MLA_QK = 2 * LANES


def _params(semantics, vmem_mib):
    return pltpu.CompilerParams(dimension_semantics=semantics, vmem_limit_bytes=vmem_mib * MIB)


def _rope_table_kernel(pos_ref, inv_ref, mult_ref, *out_refs, kinds):
    ang = pos_ref[...] * inv_ref[...]
    c = jnp.cos(ang)
    s = jnp.sin(ang)
    for i, (kind, o_ref) in enumerate(zip(kinds, out_refs)):
        o_ref[...] = (c if kind == "cos" else s) * mult_ref[i:i + 1, :]


def _rope_tables(pos_f32, inv_row, mult_rows, kinds, tm=2048):
    n = pos_f32.shape[0]
    nk = len(kinds)
    return pl.pallas_call(
        functools.partial(_rope_table_kernel, kinds=kinds),
        grid=(n // tm,),
        in_specs=[pl.BlockSpec((tm, 1), lambda i: (i, 0)),
                  pl.BlockSpec((1, LANES), lambda i: (0, 0)),
                  pl.BlockSpec((nk, LANES), lambda i: (0, 0))],
        out_specs=[pl.BlockSpec((tm, LANES), lambda i: (i, 0))] * nk,
        out_shape=[jax.ShapeDtypeStruct((n, LANES), F32)] * nk,
        compiler_params=_params(("arbitrary",), 32),
        name="rope_tables",
    )(pos_f32, inv_row, mult_rows)


def _cast_perm_kernel(x_ref, *refs, dilations):
    o_refs, col_scr = refs[:-1], refs[-1]
    tm, dm = x_ref.shape[1], x_ref.shape[2]
    strided = any(d != 1 for d in dilations)
    for c in range(dm // LANES):
        cs = slice(c * LANES, (c + 1) * LANES)
        if strided:
            col_scr[c] = x_ref[0, :, cs]
        for d, o_ref in zip(dilations, o_refs):
            if d == 1:
                o_ref[0, 0, :, cs] = x_ref[0, :, cs].astype(BF16)
            else:
                for r in range(d):
                    o_ref[0, r, :, cs] = col_scr[c, pl.ds(r, tm // d, stride=d), :].astype(BF16)


def _cast_perm(x3, dilations, tm=1024):
    b, s, dm = x3.shape
    outs = pl.pallas_call(
        functools.partial(_cast_perm_kernel, dilations=dilations),
        grid=(b, s // tm),
        in_specs=[pl.BlockSpec((1, tm, dm), lambda bi, t: (bi, t, 0))],
        out_specs=[pl.BlockSpec((1, d, tm // d, dm), lambda bi, t: (bi, 0, t, 0)) for d in dilations],
        out_shape=[jax.ShapeDtypeStruct((b, d, s // d, dm), BF16) for d in dilations],
        scratch_shapes=[pltpu.VMEM((dm // LANES, tm, LANES), F32)],
        compiler_params=_params(("arbitrary",) * 2, 48),
        name="cast_perm",
    )(x3)
    return [o.reshape(b * s, dm) for o in outs]


def _mm_kernel(a_ref, w_ref, o_ref):
    o_ref[...] = jnp.dot(a_ref[...], w_ref[...], preferred_element_type=F32).astype(o_ref.dtype)


def _mm(a, w, col0=0, ncols=None, out_dtype=BF16, tm=1024, tn=1024):
    m, k = a.shape
    n = w.shape[1] - col0 if ncols is None else ncols
    j0 = col0 // tn
    return pl.pallas_call(
        _mm_kernel,
        grid=(m // tm, n // tn),
        in_specs=[pl.BlockSpec((tm, k), lambda i, j: (i, 0)),
                  pl.BlockSpec((k, tn), lambda i, j: (0, j + j0))],
        out_specs=pl.BlockSpec((tm, tn), lambda i, j: (i, j)),
        out_shape=jax.ShapeDtypeStruct((m, n), out_dtype),
        compiler_params=_params(("arbitrary", "arbitrary"), 48),
        name="matmul",
    )(a, w)


def _mm_rope_kernel(a_ref, w_ref, cos_ref, sin_ref, o_ref, *, n_q_blocks, q_scale):
    acc = jnp.dot(a_ref[...], w_ref[...], preferred_element_type=F32)
    scale = jnp.where(pl.program_id(1) < n_q_blocks, q_scale, 1.0).astype(F32)
    cos = cos_ref[...] * scale
    sin = sin_ref[...] * scale
    for c in range(acc.shape[1] // LANES):
        xs = acc[:, c * LANES:(c + 1) * LANES]
        o_ref[:, c * LANES:(c + 1) * LANES] = (
            xs * cos + pltpu.roll(xs, LANES // 2, 1) * sin).astype(o_ref.dtype)


def _mm_rope(a, w, col0, cos, sin, q_scale, tm=1024, tn=1024):
    m, k = a.shape
    n = 2 * DSA_WIDTH
    j0 = col0 // tn
    kern = functools.partial(_mm_rope_kernel, n_q_blocks=DSA_WIDTH // tn, q_scale=q_scale)
    return pl.pallas_call(
        kern,
        grid=(m // tm, n // tn),
        in_specs=[pl.BlockSpec((tm, k), lambda i, j: (i, 0)),
                  pl.BlockSpec((k, tn), lambda i, j: (0, j + j0)),
                  pl.BlockSpec((tm, LANES), lambda i, j: (i, 0)),
                  pl.BlockSpec((tm, LANES), lambda i, j: (i, 0))],
        out_specs=pl.BlockSpec((tm, tn), lambda i, j: (i, j)),
        out_shape=jax.ShapeDtypeStruct((m, n), BF16),
        compiler_params=_params(("arbitrary", "arbitrary"), 48),
        name="matmul_rope",
    )(a, w, cos, sin)


def _band_attention(q, kwin, vwin, bias):
    s = lax.dot_general(q, kwin, (((1,), (1,)), ((), ())), preferred_element_type=F32) + bias
    m = jnp.max(s, axis=1, keepdims=True)
    p = jnp.exp2(s - m)
    l = jnp.sum(p, axis=1, keepdims=True)
    o = jnp.dot(p.astype(BF16), vwin, preferred_element_type=F32) * (1.0 / l)
    lse2 = m + jnp.log2(l)
    return o, jnp.broadcast_to(lse2, o.shape)


def _dsa_attn_kernel(*refs, dilations, spans, heads, unroll):
    ng = len(dilations)
    qkv_refs = refs[:3 * ng]
    z_ref, o_ref = refs[3 * ng], refs[3 * ng + 1]
    win_refs = refs[3 * ng + 2:3 * ng + 2 + 2 * ng]
    bias_ref, o_scr, l_scr = refs[3 * ng + 2 + 2 * ng:]
    blk = DSA_BLOCK
    tile = z_ref.shape[1]
    t = pl.program_id(2)

    for g in range(ng):
        rows = tile // dilations[g]
        for src, dst in ((qkv_refs[3 * g + 1], win_refs[2 * g]), (qkv_refs[3 * g + 2], win_refs[2 * g + 1])):
            @pl.when(t == 0)
            def _(dst=dst):
                dst[:, 0:blk, :] = jnp.zeros((dst.shape[0], blk, dst.shape[2]), dst.dtype)

            @pl.when(t > 0)
            def _(dst=dst, rows=rows):
                dst[:, 0:blk, :] = dst[:, rows:rows + blk, :]

            dst[:, blk:blk + rows, :] = src[0]

    row = lax.broadcasted_iota(jnp.int32, (blk, 2 * blk), 0)
    col = lax.broadcasted_iota(jnp.int32, (blk, 2 * blk), 1)
    dist = row + blk - col
    for g in range(ng):
        band = (dist >= 0) & (dist <= spans[g])
        bias_ref[g, 0] = jnp.where(band, 0.0, -jnp.inf).astype(F32)
        bias_ref[g, 1] = jnp.where(band & (col >= blk), 0.0, -jnp.inf).astype(F32)

    n_items = tile // blk
    for g in range(ng):
        d = dilations[g]
        q_ref = qkv_refs[3 * g]
        kw_ref, vw_ref = win_refs[2 * g:2 * g + 2]
        nb = n_items // d

        def items(it, carry, g=g, d=d, q_ref=q_ref, kw_ref=kw_ref, vw_ref=vw_ref, nb=nb):
            for u in range(unroll):
                idx = it * unroll + u
                r = idx // nb
                bb = idx % nb
                r0 = pl.multiple_of(bb * blk, blk)
                first = jnp.logical_and(t == 0, bb == 0).astype(jnp.int32)
                bias = bias_ref[g, first]
                start = r0 * d + r
                rows_idx = pl.ds(start, blk) if d == 1 else pl.ds(start, blk, stride=d)
                for hh in range(heads):
                    ln = slice(hh * LANES, (hh + 1) * LANES)
                    o, lse = _band_attention(q_ref[0, r, pl.ds(r0, blk), ln],
                                             kw_ref[r, pl.ds(r0, 2 * blk), ln],
                                             vw_ref[r, pl.ds(r0, 2 * blk), ln], bias)
                    o_scr[hh, g, rows_idx, :] = o
                    l_scr[hh, g, rows_idx, :] = lse
            return carry

        lax.fori_loop(0, n_items // unroll, items, 0)

    chunk = 256

    def finish(c, carry):
        r0 = pl.multiple_of(c * chunk, chunk)
        for hh in range(heads):
            ln = slice(hh * LANES, (hh + 1) * LANES)
            ls = [l_scr[hh, g, pl.ds(r0, chunk), :] for g in range(ng)]
            mx = functools.reduce(jnp.maximum, ls)
            es = [jnp.exp2(l - mx) for l in ls]
            den = functools.reduce(lambda a, b: a + b, es)
            num = functools.reduce(
                lambda a, b: a + b, [es[g] * o_scr[hh, g, pl.ds(r0, chunk), :] for g in range(ng)])
            zz = z_ref[0, pl.ds(r0, chunk), ln].astype(F32)
            y = (num / den) * (zz / (1.0 + jnp.exp(-zz)))
            o_ref[0, pl.ds(r0, chunk), ln] = y.astype(o_ref.dtype)
        return carry

    lax.fori_loop(0, tile // chunk, finish, 0)


def _dsa_attention(hs, z, batch, seq):
    heads = DSA_HEADS_PER_STEP
    wl = heads * LANES
    ncol = DSA_WIDTH // wl
    tile = DSA_TILE
    dil = tuple(d for _, d in DSA_PATTERNS)
    spans = tuple(w // d for w, d in DSA_PATTERNS)
    for d, sp in zip(dil, spans):
        assert sp <= DSA_BLOCK and tile % (d * DSA_BLOCK) == 0 and seq % tile == 0
    assert (tile // DSA_BLOCK) % DSA_BLOCKS_PER_ITER == 0

    in_specs, args, scratch = [], [], []
    for g, d in enumerate(dil):
        hqk, hv = hs[g]
        for kind in range(3):
            in_specs.append(pl.BlockSpec(
                (1, d, tile // d, wl), lambda b, h, t, kind=kind: (b, 0, t, (kind % 2) * ncol + h)))
            src = hv if kind == 2 else hqk
            args.append(src.reshape(batch, d, seq // d, src.shape[1]))
        scratch += [pltpu.VMEM((d, DSA_BLOCK + tile // d, wl), BF16)] * 2
    in_specs.append(pl.BlockSpec((1, tile, wl), lambda b, h, t: (b, t, h)))
    args.append(z.reshape(batch, seq, DSA_WIDTH))
    scratch.append(pltpu.VMEM((len(dil), 2, DSA_BLOCK, 2 * DSA_BLOCK), F32))
    scratch += [pltpu.VMEM((heads, len(dil), tile, LANES), F32)] * 2

    out = pl.pallas_call(
        functools.partial(_dsa_attn_kernel, dilations=dil, spans=spans, heads=heads,
                          unroll=DSA_BLOCKS_PER_ITER),
        grid=(batch, ncol, seq // tile),
        in_specs=in_specs,
        out_specs=pl.BlockSpec((1, tile, wl), lambda b, h, t: (b, t, h)),
        out_shape=jax.ShapeDtypeStruct((batch, seq, DSA_WIDTH), BF16),
        scratch_shapes=scratch,
        compiler_params=_params(("arbitrary",) * 3, 56),
        name="dsa_attention",
    )(*args)
    return out.reshape(batch * seq, DSA_WIDTH)


def _out_ln_kernel(y_ref, w_ref, x_ref, g_ref, b_ref, o_ref, ob_ref):
    r = DEEPNORM_ALPHA * x_ref[...] + jnp.dot(y_ref[...], w_ref[...], preferred_element_type=F32)
    mu = jnp.mean(r, axis=-1, keepdims=True)
    xc = r - mu
    var = jnp.mean(xc * xc, axis=-1, keepdims=True)
    out = xc * lax.rsqrt(var + LN_EPS) * g_ref[...] + b_ref[...]
    o_ref[...] = out
    ob_ref[...] = out.astype(BF16)


def _out_ln(y, w, x, g, b, tm=512):
    m, k = y.shape
    n = w.shape[1]
    row = lambda i: (i, 0)
    fixed = lambda i: (0, 0)
    return pl.pallas_call(
        _out_ln_kernel,
        grid=(m // tm,),
        in_specs=[pl.BlockSpec((tm, k), row), pl.BlockSpec((k, n), fixed, pipeline_mode=pl.Buffered(1)),
                  pl.BlockSpec((tm, n), row), pl.BlockSpec((1, n), fixed), pl.BlockSpec((1, n), fixed)],
        out_specs=[pl.BlockSpec((tm, n), row), pl.BlockSpec((tm, n), row)],
        out_shape=[jax.ShapeDtypeStruct((m, n), F32), jax.ShapeDtypeStruct((m, n), BF16)],
        compiler_params=_params(("arbitrary",), 56),
        name="out_proj_layernorm",
    )(y, w, x, g, b)


def _mla_proj_kernel(x_ref, w1_ref, wqn_ref, wqp_ref, wkn_ref, wvt_ref, qg_ref, kvg_ref,
                     cos_ref, sa_ref, sb_ref, q_ref, k_ref, vt_ref, *, scale):
    h1 = jnp.dot(x_ref[0], w1_ref[...], preferred_element_type=F32)
    cq = h1[:, :MLA_Q_RANK]
    ckv = h1[:, MLA_Q_RANK:MLA_Q_RANK + MLA_KV_RANK]
    kpe = h1[:, MLA_Q_RANK + MLA_KV_RANK:]

    def rms(c, g):
        y = c * lax.rsqrt(jnp.mean(c * c, axis=-1, keepdims=True) + RMS_EPS)
        return (y * g).astype(BF16)

    cos, sa, sb = cos_ref[0], sa_ref[0], sb_ref[0]

    def rope(xs):
        return xs * cos + pltpu.roll(xs, MLA_ROPE // 2, 1) * sa + pltpu.roll(xs, LANES - MLA_ROPE // 2, 1) * sb

    cqn = rms(cq, qg_ref[...])
    ckvn = rms(ckv, kvg_ref[...])
    kpe_r = rope(kpe).astype(BF16)
    qn = jnp.dot(cqn, wqn_ref[...], preferred_element_type=F32)
    qp = jnp.dot(cqn, wqp_ref[...], preferred_element_type=F32)
    kn = jnp.dot(ckvn, wkn_ref[...], preferred_element_type=F32)
    vt = lax.dot_general(wvt_ref[...], ckvn, (((1,), (1,)), ((), ())), preferred_element_type=F32)
    extra = MLA_VT_ROWS - MLA_V
    ones_row = (lax.broadcasted_iota(jnp.int32, (extra, vt.shape[1]), 0) == 0).astype(BF16)
    for h in range(MLA_HEADS):
        vt_ref[0, h, MLA_V:MLA_VT_ROWS, :] = ones_row
        hs = slice(h * LANES, (h + 1) * LANES)
        q_ref[0, h, :, 0:LANES] = (qn[:, hs] * scale).astype(BF16)
        q_ref[0, h, :, LANES:MLA_QK] = (rope(qp[:, hs]) * scale).astype(BF16)
        k_ref[0, h, :, 0:LANES] = kn[:, hs].astype(BF16)
        k_ref[0, h, :, LANES:MLA_QK] = kpe_r
        vt_ref[0, h, 0:MLA_V, :] = vt[h * MLA_V:(h + 1) * MLA_V, :].astype(BF16)


def _mla_proj(xb, w1, wqn, wqp, wkn, wvt, qg, kvg, cos, sa, sb, scale, batch, seq, tm=256):
    dm = xb.shape[1]
    row = lambda b, i: (b, i, 0)
    fixed = lambda b, i: (0, 0)
    wspec = lambda w: pl.BlockSpec(w.shape, fixed, pipeline_mode=pl.Buffered(1))
    tab = lambda a: a.reshape(batch, seq, LANES)
    return pl.pallas_call(
        functools.partial(_mla_proj_kernel, scale=scale),
        grid=(batch, seq // tm),
        in_specs=[pl.BlockSpec((1, tm, dm), row), wspec(w1), wspec(wqn), wspec(wqp), wspec(wkn), wspec(wvt),
                  wspec(qg), wspec(kvg)] + [pl.BlockSpec((1, tm, LANES), row)] * 3,
        out_specs=[pl.BlockSpec((1, MLA_HEADS, tm, MLA_QK), lambda b, i: (b, 0, i, 0)),
                   pl.BlockSpec((1, MLA_HEADS, tm, MLA_QK), lambda b, i: (b, 0, i, 0)),
                   pl.BlockSpec((1, MLA_HEADS, MLA_VT_ROWS, tm), lambda b, i: (b, 0, 0, i))],
        out_shape=[jax.ShapeDtypeStruct((batch, MLA_HEADS, seq, MLA_QK), BF16),
                   jax.ShapeDtypeStruct((batch, MLA_HEADS, seq, MLA_QK), BF16),
                   jax.ShapeDtypeStruct((batch, MLA_HEADS, MLA_VT_ROWS, seq), BF16)],
        compiler_params=_params(("arbitrary",) * 2, 56),
        name="mla_projections",
    )(xb.reshape(batch, seq, dm), w1, wqn, wqp, wkn, wvt, qg, kvg, tab(cos), tab(sa), tab(sb))


def _mla_flash_kernel(qi_ref, ki_ref, q_ref, k_ref, vt_ref, z_ref, o_ref, s_buf, m_scr, acc_scr, *, heads):
    t = pl.program_id(2)
    qi = qi_ref[t]
    ki = ki_ref[t]
    tq = q_ref.shape[2]
    tk = k_ref.shape[2]

    @pl.when(ki == 0)
    def _():
        m_scr[...] = jnp.full(m_scr.shape, -jnp.inf, F32)
        acc_scr[...] = jnp.zeros(acc_scr.shape, F32)

    def scores(h):
        return lax.dot_general(k_ref[0, h], q_ref[0, h], (((1,), (1,)), ((), ())), preferred_element_type=F32)

    def softmax_pv(h, s, masked):
        if masked:
            key = lax.broadcasted_iota(jnp.int32, (tk, tq), 0)
            qry = lax.broadcasted_iota(jnp.int32, (tk, tq), 1)
            s = jnp.where(qry >= key, s, -jnp.inf)
        m_prev = m_scr[h]
        m_new = jnp.maximum(m_prev, jnp.max(s, axis=0, keepdims=True))
        alpha = jnp.exp2(m_prev - m_new)
        p = jnp.exp2(s - m_new)
        m_scr[h] = m_new
        acc_scr[h] = alpha * acc_scr[h] + jnp.dot(vt_ref[0, h], p.astype(BF16), preferred_element_type=F32)

    def update(masked):
        s_buf[0] = scores(0)
        for h in range(heads):
            if h + 1 < heads:
                s_buf[(h + 1) % 2] = scores(h + 1)
            softmax_pv(h, s_buf[h % 2], masked)

    @pl.when(ki < qi)
    def _():
        update(False)

    @pl.when(ki == qi)
    def _():
        update(True)
        for h in range(heads):
            ln = slice(h * MLA_V, (h + 1) * MLA_V)
            zz = z_ref[0, :, ln].astype(F32)
            acc = acc_scr[h]
            o = (acc[:MLA_V] * (1.0 / acc[MLA_V:MLA_V + 1])).T
            o_ref[0, :, ln] = (o * (zz / (1.0 + jnp.exp(-zz)))).astype(o_ref.dtype)


def _mla_flash(q, k, vt, z, batch, seq):
    heads = MLA_HEADS_PER_STEP
    tile = MLA_TILE
    nt = seq // tile
    qi = np.concatenate([np.full(i + 1, i, np.int32) for i in range(nt)])
    ki = np.concatenate([np.arange(i + 1, dtype=np.int32) for i in range(nt)])
    v_w = heads * MLA_V
    qmap = lambda b, h, t, qi_r, ki_r: (b, h, qi_r[t], 0)
    kmap = lambda b, h, t, qi_r, ki_r: (b, h, ki_r[t], 0)
    vmap = lambda b, h, t, qi_r, ki_r: (b, h, 0, ki_r[t])
    zmap = lambda b, h, t, qi_r, ki_r: (b, qi_r[t], h)
    grid_spec = pltpu.PrefetchScalarGridSpec(
        num_scalar_prefetch=2,
        grid=(batch, MLA_HEADS // heads, len(qi)),
        in_specs=[pl.BlockSpec((1, heads, tile, MLA_QK), qmap), pl.BlockSpec((1, heads, tile, MLA_QK), kmap),
                  pl.BlockSpec((1, heads, MLA_VT_ROWS, tile), vmap), pl.BlockSpec((1, tile, v_w), zmap)],
        out_specs=pl.BlockSpec((1, tile, v_w), zmap),
        scratch_shapes=[pltpu.VMEM((2, tile, tile), F32), pltpu.VMEM((heads, 1, tile), F32),
                        pltpu.VMEM((heads, MLA_VT_ROWS, tile), F32)],
    )
    out = pl.pallas_call(
        functools.partial(_mla_flash_kernel, heads=heads),
        grid_spec=grid_spec,
        out_shape=jax.ShapeDtypeStruct((batch, seq, MLA_WIDTH), BF16),
        compiler_params=_params(("arbitrary",) * 3, 56),
        name="mla_flash_attention",
    )(jnp.asarray(qi), jnp.asarray(ki), q, k, vt, z.reshape(batch, seq, MLA_WIDTH))
    return out.reshape(batch * seq, MLA_WIDTH)


def _inv_freq(dim):
    return 1.0 / (ROPE_THETA ** (jnp.arange(0, dim, 2, dtype=F32) / dim))


def kernel(x, positions, dsa_w_in, dsa_w_out, mla_w_in, mla_q_norm, mla_w_uq, mla_kv_norm, mla_w_ukv,
           mla_w_out, ln_g, ln_b):
    batch, seq, dm = x.shape
    tokens = batch * seq
    half = DSA_HEAD_DIM // 2
    dils = tuple(d for _, d in DSA_PATTERNS)

    inv_a = _inv_freq(DSA_HEAD_DIM)
    inv_row_a = jnp.concatenate([inv_a, inv_a])[None, :]
    mult_a = jnp.stack([jnp.ones((LANES,), F32),
                        jnp.concatenate([-jnp.ones((half,), F32), jnp.ones((half,), F32)])])
    tables_a = []
    for d in dils:
        pos_d = positions.reshape(batch, seq // d, d).transpose(0, 2, 1).reshape(tokens, 1).astype(F32)
        tables_a.append(_rope_tables(pos_d, inv_row_a, mult_a, ("cos", "sin")))

    inv_b = _inv_freq(MLA_ROPE)
    hb = MLA_ROPE // 2
    zeros_h = jnp.zeros((hb,), F32)
    ones_h = jnp.ones((hb,), F32)
    pad = jnp.zeros((LANES - MLA_ROPE,), F32)
    inv_row_b = jnp.concatenate([inv_b, inv_b, pad])[None, :]
    mult_b = jnp.stack([jnp.concatenate([ones_h, ones_h, pad]),
                        jnp.concatenate([zeros_h, ones_h, pad]),
                        jnp.concatenate([-ones_h, zeros_h, pad])])
    cos_b, sa_b, sb_b = _rope_tables(positions.reshape(tokens, 1).astype(F32), inv_row_b, mult_b,
                                     ("cos", "sin", "sin"))

    x2 = x.reshape(tokens, dm)
    xb = None
    qkv_w = 3 * DSA_WIDTH
    for layer in range(DEPTH):
        j = layer // N_MIXERS
        x3 = x2.reshape(batch, seq, dm)
        if layer % N_MIXERS == 0:
            w_in = dsa_w_in[j].astype(BF16)
            if xb is None:
                xbs = dict(zip(dils, _cast_perm(x3, dils)))
            else:
                rest = tuple(d for d in dils if d != 1)
                xbs = dict(zip(rest, _cast_perm(x3, rest)))
                xbs[1] = xb
            hs = []
            q_scale = DSA_HEAD_DIM ** -0.5 * math.log2(math.e)
            for g, d in enumerate(dils):
                cos, sin = tables_a[g]
                hs.append((_mm_rope(xbs[d], w_in, g * qkv_w, cos, sin, q_scale),
                           _mm(xbs[d], w_in, g * qkv_w + 2 * DSA_WIDTH, DSA_WIDTH)))
            z = _mm(xbs[1], w_in, DSA_GROUPS * qkv_w)
            y = _dsa_attention(hs, z, batch, seq)
            w_out = dsa_w_out[j].astype(BF16)
        else:
            if xb is None:
                xb = _cast_perm(x3, (1,))[0]
            w_in = mla_w_in[j]
            o2 = MLA_Q_RANK + MLA_KV_RANK
            o3 = o2 + MLA_ROPE
            w1 = jnp.pad(w_in[:, :o3], ((0, 0), (0, LANES - MLA_ROPE))).astype(BF16)
            wz = w_in[:, o3:].astype(BF16)
            wq = mla_w_uq[j].reshape(MLA_Q_RANK, MLA_HEADS, MLA_NOPE + MLA_ROPE)
            wqn = wq[:, :, :MLA_NOPE].reshape(MLA_Q_RANK, MLA_HEADS * MLA_NOPE).astype(BF16)
            wqp = jnp.pad(wq[:, :, MLA_NOPE:], ((0, 0), (0, 0), (0, LANES - MLA_ROPE))).reshape(
                MLA_Q_RANK, MLA_HEADS * LANES).astype(BF16)
            wkv = mla_w_ukv[j].reshape(MLA_KV_RANK, MLA_HEADS, MLA_NOPE + MLA_V)
            wkn = wkv[:, :, :MLA_NOPE].reshape(MLA_KV_RANK, MLA_HEADS * MLA_NOPE).astype(BF16)
            wvt = wkv[:, :, MLA_NOPE:].reshape(MLA_KV_RANK, MLA_HEADS * MLA_V).T.astype(BF16)
            q_scale = (MLA_NOPE + MLA_ROPE) ** -0.5 * math.log2(math.e)
            q, k, vt = _mla_proj(xb, w1, wqn, wqp, wkn, wvt, mla_q_norm[j][None, :], mla_kv_norm[j][None, :],
                                 cos_b, sa_b, sb_b, q_scale, batch, seq)
            z = _mm(xb, wz)
            y = _mla_flash(q, k, vt, z, batch, seq)
            w_out = mla_w_out[j].astype(BF16)
        x2, xb = _out_ln(y, w_out, x2, ln_g[layer][None, :], ln_b[layer][None, :])
    return x2.reshape(batch, seq, dm)
```

```python
import functools
import math

import jax
import jax.numpy as jnp
import numpy as np
from jax import lax
from jax.experimental import pallas as pl
from jax.experimental.pallas import tpu as pltpu

F32 = jnp.float32
BF16 = jnp.bfloat16

D_MODEL = 2048
DEPTH = 4
N_MIXERS = 2

DSA_PATTERNS = ((128, 1), (512, 4), (2048, 16))
DSA_GROUPS = len(DSA_PATTERNS)
DSA_HEADS = 16
DSA_HEAD_DIM = 128
DSA_WIDTH = DSA_HEADS * DSA_HEAD_DIM
DSA_BLOCK = 128

MLA_HEADS = 16
MLA_Q_RANK = 512
MLA_KV_RANK = 512
MLA_NOPE = 128
MLA_ROPE = 64
MLA_V = 128
MLA_WIDTH = MLA_HEADS * MLA_V

ROPE_THETA = 10000.0
RMS_EPS = 1e-6
LN_EPS = 1e-5
DEEPNORM_ALPHA = (2 * DEPTH) ** 0.25

LANES = 128
MIB = 1024 * 1024

DSA_TILE = 2048
DSA_HEADS_PER_STEP = 2
DSA_BLOCKS_PER_ITER = 8
MLA_TILE = 1024
MLA_HEADS_PER_STEP = 8
MLA_VT_ROWS = MLA_V + 16
MLA_QK = 2 * LANES


def _params(semantics, vmem_mib):
    return pltpu.CompilerParams(dimension_semantics=semantics, vmem_limit_bytes=vmem_mib * MIB)


def _rope_table_kernel(pos_ref, inv_ref, mult_ref, *out_refs, kinds):
    ang = pos_ref[...] * inv_ref[...]
    c = jnp.cos(ang)
    s = jnp.sin(ang)
    for i, (kind, o_ref) in enumerate(zip(kinds, out_refs)):
        o_ref[...] = (c if kind == "cos" else s) * mult_ref[i:i + 1, :]


def _rope_tables(pos_f32, inv_row, mult_rows, kinds, tm=2048):
    n = pos_f32.shape[0]
    nk = len(kinds)
    return pl.pallas_call(
        functools.partial(_rope_table_kernel, kinds=kinds),
        grid=(n // tm,),
        in_specs=[pl.BlockSpec((tm, 1), lambda i: (i, 0)),
                  pl.BlockSpec((1, LANES), lambda i: (0, 0)),
                  pl.BlockSpec((nk, LANES), lambda i: (0, 0))],
        out_specs=[pl.BlockSpec((tm, LANES), lambda i: (i, 0))] * nk,
        out_shape=[jax.ShapeDtypeStruct((n, LANES), F32)] * nk,
        compiler_params=_params(("arbitrary",), 32),
        name="rope_tables",
    )(pos_f32, inv_row, mult_rows)


def _cast_perm_kernel(x_ref, *refs, dilations):
    o_refs, col_scr = refs[:-1], refs[-1]
    tm, dm = x_ref.shape[1], x_ref.shape[2]
    strided = any(d != 1 for d in dilations)
    for c in range(dm // LANES):
        cs = slice(c * LANES, (c + 1) * LANES)
        if strided:
            col_scr[c] = x_ref[0, :, cs]
        for d, o_ref in zip(dilations, o_refs):
            if d == 1:
                o_ref[0, 0, :, cs] = x_ref[0, :, cs].astype(BF16)
            else:
                for r in range(d):
                    o_ref[0, r, :, cs] = col_scr[c, pl.ds(r, tm // d, stride=d), :].astype(BF16)


def _cast_perm(x3, dilations, tm=1024):
    b, s, dm = x3.shape
    outs = pl.pallas_call(
        functools.partial(_cast_perm_kernel, dilations=dilations),
        grid=(b, s // tm),
        in_specs=[pl.BlockSpec((1, tm, dm), lambda bi, t: (bi, t, 0))],
        out_specs=[pl.BlockSpec((1, d, tm // d, dm), lambda bi, t: (bi, 0, t, 0)) for d in dilations],
        out_shape=[jax.ShapeDtypeStruct((b, d, s // d, dm), BF16) for d in dilations],
        scratch_shapes=[pltpu.VMEM((dm // LANES, tm, LANES), F32)],
        compiler_params=_params(("arbitrary",) * 2, 48),
        name="cast_perm",
    )(x3)
    return [o.reshape(b * s, dm) for o in outs]


def _mm_kernel(a_ref, w_ref, o_ref):
    o_ref[...] = jnp.dot(a_ref[...], w_ref[...], preferred_element_type=F32).astype(o_ref.dtype)


def _mm(a, w, col0=0, ncols=None, out_dtype=BF16, tm=2048, tn=1024):
    m, k = a.shape
    n = w.shape[1] - col0 if ncols is None else ncols
    j0 = col0 // tn
    return pl.pallas_call(
        _mm_kernel,
        grid=(m // tm, n // tn),
        in_specs=[pl.BlockSpec((tm, k), lambda i, j: (i, 0)),
                  pl.BlockSpec((k, tn), lambda i, j: (0, j + j0))],
        out_specs=pl.BlockSpec((tm, tn), lambda i, j: (i, j)),
        out_shape=jax.ShapeDtypeStruct((m, n), out_dtype),
        compiler_params=_params(("arbitrary", "arbitrary"), 56),
        name="matmul",
    )(a, w)


def _mm_rope_kernel(a_ref, w_ref, cos_ref, sin_ref, o_ref, *, n_q_blocks, q_scale):
    acc = jnp.dot(a_ref[...], w_ref[...], preferred_element_type=F32)
    scale = jnp.where(pl.program_id(1) < n_q_blocks, q_scale, 1.0).astype(F32)
    cos = cos_ref[...] * scale
    sin = sin_ref[...] * scale
    for c in range(acc.shape[1] // LANES):
        xs = acc[:, c * LANES:(c + 1) * LANES]
        o_ref[:, c * LANES:(c + 1) * LANES] = (
            xs * cos + pltpu.roll(xs, LANES // 2, 1) * sin).astype(o_ref.dtype)


def _mm_rope(a, w, col0, cos, sin, q_scale, tm=2048, tn=1024):
    m, k = a.shape
    n = 2 * DSA_WIDTH
    j0 = col0 // tn
    kern = functools.partial(_mm_rope_kernel, n_q_blocks=DSA_WIDTH // tn, q_scale=q_scale)
    return pl.pallas_call(
        kern,
        grid=(m // tm, n // tn),
        in_specs=[pl.BlockSpec((tm, k), lambda i, j: (i, 0)),
                  pl.BlockSpec((k, tn), lambda i, j: (0, j + j0)),
                  pl.BlockSpec((tm, LANES), lambda i, j: (i, 0)),
                  pl.BlockSpec((tm, LANES), lambda i, j: (i, 0))],
        out_specs=pl.BlockSpec((tm, tn), lambda i, j: (i, j)),
        out_shape=jax.ShapeDtypeStruct((m, n), BF16),
        compiler_params=_params(("arbitrary", "arbitrary"), 56),
        name="matmul_rope",
    )(a, w, cos, sin)


def _band_attention(q, kwin, vwin, bias):
    s = lax.dot_general(q, kwin, (((1,), (1,)), ((), ())), preferred_element_type=F32) + bias
    m = jnp.max(s, axis=1, keepdims=True)
    p = jnp.exp2(s - m)
    l = jnp.sum(p, axis=1, keepdims=True)
    o = jnp.dot(p.astype(BF16), vwin, preferred_element_type=F32) * (1.0 / l)
    lse2 = m + jnp.log2(l)
    return o, jnp.broadcast_to(lse2, o.shape)


def _dsa_attn_kernel(*refs, dilations, spans, heads, unroll):
    ng = len(dilations)
    qkv_refs = refs[:3 * ng]
    z_ref, o_ref = refs[3 * ng], refs[3 * ng + 1]
    win_refs = refs[3 * ng + 2:3 * ng + 2 + 2 * ng]
    bias_ref, o_scr, l_scr = refs[3 * ng + 2 + 2 * ng:]
    blk = DSA_BLOCK
    tile = z_ref.shape[1]
    t = pl.program_id(2)

    for g in range(ng):
        rows = tile // dilations[g]
        for src, dst in ((qkv_refs[3 * g + 1], win_refs[2 * g]), (qkv_refs[3 * g + 2], win_refs[2 * g + 1])):
            @pl.when(t == 0)
            def _(dst=dst):
                dst[:, 0:blk, :] = jnp.zeros((dst.shape[0], blk, dst.shape[2]), dst.dtype)

            @pl.when(t > 0)
            def _(dst=dst, rows=rows):
                dst[:, 0:blk, :] = dst[:, rows:rows + blk, :]

            dst[:, blk:blk + rows, :] = src[0]

    row = lax.broadcasted_iota(jnp.int32, (blk, 2 * blk), 0)
    col = lax.broadcasted_iota(jnp.int32, (blk, 2 * blk), 1)
    dist = row + blk - col
    for g in range(ng):
        band = (dist >= 0) & (dist <= spans[g])
        bias_ref[g, 0] = jnp.where(band, 0.0, -jnp.inf).astype(F32)
        bias_ref[g, 1] = jnp.where(band & (col >= blk), 0.0, -jnp.inf).astype(F32)

    n_items = tile // blk
    for g in range(ng):
        d = dilations[g]
        q_ref = qkv_refs[3 * g]
        kw_ref, vw_ref = win_refs[2 * g:2 * g + 2]
        nb = n_items // d

        def items(it, carry, g=g, d=d, q_ref=q_ref, kw_ref=kw_ref, vw_ref=vw_ref, nb=nb):
            for u in range(unroll):
                idx = it * unroll + u
                r = idx // nb
                bb = idx % nb
                r0 = pl.multiple_of(bb * blk, blk)
                first = jnp.logical_and(t == 0, bb == 0).astype(jnp.int32)
                bias = bias_ref[g, first]
                start = r0 * d + r
                rows_idx = pl.ds(start, blk) if d == 1 else pl.ds(start, blk, stride=d)
                for hh in range(heads):
                    ln = slice(hh * LANES, (hh + 1) * LANES)
                    o, lse = _band_attention(q_ref[0, r, pl.ds(r0, blk), ln],
                                             kw_ref[r, pl.ds(r0, 2 * blk), ln],
                                             vw_ref[r, pl.ds(r0, 2 * blk), ln], bias)
                    o_scr[hh, g, rows_idx, :] = o
                    l_scr[hh, g, rows_idx, :] = lse
            return carry

        lax.fori_loop(0, n_items // unroll, items, 0)

    chunk = 256

    def finish(c, carry):
        r0 = pl.multiple_of(c * chunk, chunk)
        for hh in range(heads):
            ln = slice(hh * LANES, (hh + 1) * LANES)
            ls = [l_scr[hh, g, pl.ds(r0, chunk), :] for g in range(ng)]
            mx = functools.reduce(jnp.maximum, ls)
            es = [jnp.exp2(l - mx) for l in ls]
            den = functools.reduce(lambda a, b: a + b, es)
            num = functools.reduce(
                lambda a, b: a + b, [es[g] * o_scr[hh, g, pl.ds(r0, chunk), :] for g in range(ng)])
            zz = z_ref[0, pl.ds(r0, chunk), ln].astype(F32)
            y = (num * zz) / (den * (1.0 + jnp.exp(-zz)))
            o_ref[0, pl.ds(r0, chunk), ln] = y.astype(o_ref.dtype)
        return carry

    lax.fori_loop(0, tile // chunk, finish, 0)


def _dsa_attention(hs, z, batch, seq):
    heads = DSA_HEADS_PER_STEP
    wl = heads * LANES
    ncol = DSA_WIDTH // wl
    tile = DSA_TILE
    dil = tuple(d for _, d in DSA_PATTERNS)
    spans = tuple(w // d for w, d in DSA_PATTERNS)
    for d, sp in zip(dil, spans):
        assert sp <= DSA_BLOCK and tile % (d * DSA_BLOCK) == 0 and seq % tile == 0
    assert (tile // DSA_BLOCK) % DSA_BLOCKS_PER_ITER == 0

    in_specs, args, scratch = [], [], []
    for g, d in enumerate(dil):
        hqk, hv = hs[g]
        for kind in range(3):
            in_specs.append(pl.BlockSpec(
                (1, d, tile // d, wl), lambda b, h, t, kind=kind: (b, 0, t, (kind % 2) * ncol + h)))
            src = hv if kind == 2 else hqk
            args.append(src.reshape(batch, d, seq // d, src.shape[1]))
        scratch += [pltpu.VMEM((d, DSA_BLOCK + tile // d, wl), BF16)] * 2
    in_specs.append(pl.BlockSpec((1, tile, wl), lambda b, h, t: (b, t, h)))
    args.append(z.reshape(batch, seq, DSA_WIDTH))
    scratch.append(pltpu.VMEM((len(dil), 2, DSA_BLOCK, 2 * DSA_BLOCK), F32))
    scratch += [pltpu.VMEM((heads, len(dil), tile, LANES), F32)] * 2

    out = pl.pallas_call(
        functools.partial(_dsa_attn_kernel, dilations=dil, spans=spans, heads=heads,
                          unroll=DSA_BLOCKS_PER_ITER),
        grid=(batch, ncol, seq // tile),
        in_specs=in_specs,
        out_specs=pl.BlockSpec((1, tile, wl), lambda b, h, t: (b, t, h)),
        out_shape=jax.ShapeDtypeStruct((batch, seq, DSA_WIDTH), BF16),
        scratch_shapes=scratch,
        compiler_params=_params(("arbitrary",) * 3, 56),
        name="dsa_attention",
    )(*args)
    return out.reshape(batch * seq, DSA_WIDTH)


def _out_ln_kernel(y_ref, w_ref, x_ref, g_ref, b_ref, o_ref, ob_ref):
    r = DEEPNORM_ALPHA * x_ref[...] + jnp.dot(y_ref[...], w_ref[...], preferred_element_type=F32)
    mu = jnp.mean(r, axis=-1, keepdims=True)
    xc = r - mu
    var = jnp.mean(xc * xc, axis=-1, keepdims=True)
    out = xc * lax.rsqrt(var + LN_EPS) * g_ref[...] + b_ref[...]
    o_ref[...] = out
    ob_ref[...] = out.astype(BF16)


def _out_ln(y, w, x, g, b, tm=512):
    m, k = y.shape
    n = w.shape[1]
    row = lambda i: (i, 0)
    fixed = lambda i: (0, 0)
    return pl.pallas_call(
        _out_ln_kernel,
        grid=(m // tm,),
        in_specs=[pl.BlockSpec((tm, k), row), pl.BlockSpec((k, n), fixed, pipeline_mode=pl.Buffered(1)),
                  pl.BlockSpec((tm, n), row), pl.BlockSpec((1, n), fixed), pl.BlockSpec((1, n), fixed)],
        out_specs=[pl.BlockSpec((tm, n), row), pl.BlockSpec((tm, n), row)],
        out_shape=[jax.ShapeDtypeStruct((m, n), F32), jax.ShapeDtypeStruct((m, n), BF16)],
        compiler_params=_params(("arbitrary",), 56),
        name="out_proj_layernorm",
    )(y, w, x, g, b)


def _mla_proj_kernel(x_ref, w1_ref, wqn_ref, wqp_ref, wkn_ref, wvt_ref, qg_ref, kvg_ref,
                     cos_ref, sa_ref, sb_ref, q_ref, k_ref, vt_ref, *, scale):
    h1 = jnp.dot(x_ref[0], w1_ref[...], preferred_element_type=F32)
    cq = h1[:, :MLA_Q_RANK]
    ckv = h1[:, MLA_Q_RANK:MLA_Q_RANK + MLA_KV_RANK]
    kpe = h1[:, MLA_Q_RANK + MLA_KV_RANK:]

    def rms(c, g):
        y = c * lax.rsqrt(jnp.mean(c * c, axis=-1, keepdims=True) + RMS_EPS)
        return (y * g).astype(BF16)

    cos, sa, sb = cos_ref[0], sa_ref[0], sb_ref[0]

    def rope(xs):
        return xs * cos + pltpu.roll(xs, MLA_ROPE // 2, 1) * sa + pltpu.roll(xs, LANES - MLA_ROPE // 2, 1) * sb

    cqn = rms(cq, qg_ref[...])
    ckvn = rms(ckv, kvg_ref[...])
    kpe_r = rope(kpe).astype(BF16)
    qn = jnp.dot(cqn, wqn_ref[...], preferred_element_type=F32)
    qp = jnp.dot(cqn, wqp_ref[...], preferred_element_type=F32)
    kn = jnp.dot(ckvn, wkn_ref[...], preferred_element_type=F32)
    vt = lax.dot_general(wvt_ref[...], ckvn, (((1,), (1,)), ((), ())), preferred_element_type=F32)
    extra = MLA_VT_ROWS - MLA_V
    ones_row = (lax.broadcasted_iota(jnp.int32, (extra, vt.shape[1]), 0) == 0).astype(BF16)
    for h in range(MLA_HEADS):
        vt_ref[0, h, MLA_V:MLA_VT_ROWS, :] = ones_row
        hs = slice(h * LANES, (h + 1) * LANES)
        q_ref[0, h, :, 0:LANES] = (qn[:, hs] * scale).astype(BF16)
        q_ref[0, h, :, LANES:MLA_QK] = (rope(qp[:, hs]) * scale).astype(BF16)
        k_ref[0, h, :, 0:LANES] = kn[:, hs].astype(BF16)
        k_ref[0, h, :, LANES:MLA_QK] = kpe_r
        vt_ref[0, h, 0:MLA_V, :] = vt[h * MLA_V:(h + 1) * MLA_V, :].astype(BF16)


def _mla_proj(xb, w1, wqn, wqp, wkn, wvt, qg, kvg, cos, sa, sb, scale, batch, seq, tm=256):
    dm = xb.shape[1]
    row = lambda b, i: (b, i, 0)
    fixed = lambda b, i: (0, 0)
    wspec = lambda w: pl.BlockSpec(w.shape, fixed, pipeline_mode=pl.Buffered(1))
    tab = lambda a: a.reshape(batch, seq, LANES)
    return pl.pallas_call(
        functools.partial(_mla_proj_kernel, scale=scale),
        grid=(batch, seq // tm),
        in_specs=[pl.BlockSpec((1, tm, dm), row), wspec(w1), wspec(wqn), wspec(wqp), wspec(wkn), wspec(wvt),
                  wspec(qg), wspec(kvg)] + [pl.BlockSpec((1, tm, LANES), row)] * 3,
        out_specs=[pl.BlockSpec((1, MLA_HEADS, tm, MLA_QK), lambda b, i: (b, 0, i, 0)),
                   pl.BlockSpec((1, MLA_HEADS, tm, MLA_QK), lambda b, i: (b, 0, i, 0)),
                   pl.BlockSpec((1, MLA_HEADS, MLA_VT_ROWS, tm), lambda b, i: (b, 0, 0, i))],
        out_shape=[jax.ShapeDtypeStruct((batch, MLA_HEADS, seq, MLA_QK), BF16),
                   jax.ShapeDtypeStruct((batch, MLA_HEADS, seq, MLA_QK), BF16),
                   jax.ShapeDtypeStruct((batch, MLA_HEADS, MLA_VT_ROWS, seq), BF16)],
        compiler_params=_params(("arbitrary",) * 2, 56),
        name="mla_projections",
    )(xb.reshape(batch, seq, dm), w1, wqn, wqp, wkn, wvt, qg, kvg, tab(cos), tab(sa), tab(sb))


def _mla_flash_kernel(qi_ref, ki_ref, q_ref, k_ref, vt_ref, z_ref, o_ref, s_buf, smax_buf, m_scr, acc_scr, *,
                      heads):
    t = pl.program_id(2)
    qi = qi_ref[t]
    ki = ki_ref[t]
    tq = q_ref.shape[2]
    tk = k_ref.shape[2]

    @pl.when(ki == 0)
    def _():
        m_scr[...] = jnp.full(m_scr.shape, -jnp.inf, F32)
        acc_scr[...] = jnp.zeros(acc_scr.shape, F32)

    def scores(h, masked):
        s = lax.dot_general(k_ref[0, h], q_ref[0, h], (((1,), (1,)), ((), ())), preferred_element_type=F32)
        if masked:
            key = lax.broadcasted_iota(jnp.int32, (tk, tq), 0)
            qry = lax.broadcasted_iota(jnp.int32, (tk, tq), 1)
            s = jnp.where(qry >= key, s, -jnp.inf)
        s_buf[h % 2] = s
        smax_buf[h % 2] = jnp.max(s, axis=0, keepdims=True)

    def softmax_pv(h):
        m_prev = m_scr[h]
        m_new = jnp.maximum(m_prev, smax_buf[h % 2])
        alpha = jnp.exp2(m_prev - m_new)
        p = jnp.exp2(s_buf[h % 2] - m_new)
        m_scr[h] = m_new
        acc_scr[h] = alpha * acc_scr[h] + jnp.dot(vt_ref[0, h], p.astype(BF16), preferred_element_type=F32)

    def update(masked):
        scores(0, masked)
        for h in range(heads):
            if h + 1 < heads:
                scores(h + 1, masked)
            softmax_pv(h)

    @pl.when(ki < qi)
    def _():
        update(False)

    @pl.when(ki == qi)
    def _():
        update(True)
        for h in range(heads):
            ln = slice(h * MLA_V, (h + 1) * MLA_V)
            zz = z_ref[0, :, ln].astype(F32)
            acc = acc_scr[h]
            o = (acc[:MLA_V] * (1.0 / acc[MLA_V:MLA_V + 1])).T
            o_ref[0, :, ln] = (o * (zz / (1.0 + jnp.exp(-zz)))).astype(o_ref.dtype)


def _mla_flash(q, k, vt, z, batch, seq):
    heads = MLA_HEADS_PER_STEP
    tile = MLA_TILE
    nt = seq // tile
    qi = np.concatenate([np.full(i + 1, i, np.int32) for i in range(nt)])
    ki = np.concatenate([np.arange(i + 1, dtype=np.int32) for i in range(nt)])
    v_w = heads * MLA_V
    qmap = lambda b, h, t, qi_r, ki_r: (b, h, qi_r[t], 0)
    kmap = lambda b, h, t, qi_r, ki_r: (b, h, ki_r[t], 0)
    vmap = lambda b, h, t, qi_r, ki_r: (b, h, 0, ki_r[t])
    zmap = lambda b, h, t, qi_r, ki_r: (b, qi_r[t], h)
    grid_spec = pltpu.PrefetchScalarGridSpec(
        num_scalar_prefetch=2,
        grid=(batch, MLA_HEADS // heads, len(qi)),
        in_specs=[pl.BlockSpec((1, heads, tile, MLA_QK), qmap), pl.BlockSpec((1, heads, tile, MLA_QK), kmap),
                  pl.BlockSpec((1, heads, MLA_VT_ROWS, tile), vmap), pl.BlockSpec((1, tile, v_w), zmap)],
        out_specs=pl.BlockSpec((1, tile, v_w), zmap),
        scratch_shapes=[pltpu.VMEM((2, tile, tile), F32), pltpu.VMEM((2, 1, tile), F32),
                        pltpu.VMEM((heads, 1, tile), F32), pltpu.VMEM((heads, MLA_VT_ROWS, tile), F32)],
    )
    out = pl.pallas_call(
        functools.partial(_mla_flash_kernel, heads=heads),
        grid_spec=grid_spec,
        out_shape=jax.ShapeDtypeStruct((batch, seq, MLA_WIDTH), BF16),
        compiler_params=_params(("arbitrary",) * 3, 56),
        name="mla_flash_attention",
    )(jnp.asarray(qi), jnp.asarray(ki), q, k, vt, z.reshape(batch, seq, MLA_WIDTH))
    return out.reshape(batch * seq, MLA_WIDTH)


def _inv_freq(dim):
    return 1.0 / (ROPE_THETA ** (jnp.arange(0, dim, 2, dtype=F32) / dim))


def kernel(x, positions, dsa_w_in, dsa_w_out, mla_w_in, mla_q_norm, mla_w_uq, mla_kv_norm, mla_w_ukv,
           mla_w_out, ln_g, ln_b):
    batch, seq, dm = x.shape
    tokens = batch * seq
    half = DSA_HEAD_DIM // 2
    dils = tuple(d for _, d in DSA_PATTERNS)

    inv_a = _inv_freq(DSA_HEAD_DIM)
    inv_row_a = jnp.concatenate([inv_a, inv_a])[None, :]
    mult_a = jnp.stack([jnp.ones((LANES,), F32),
                        jnp.concatenate([-jnp.ones((half,), F32), jnp.ones((half,), F32)])])
    tables_a = []
    for d in dils:
        pos_d = positions.reshape(batch, seq // d, d).transpose(0, 2, 1).reshape(tokens, 1).astype(F32)
        tables_a.append(_rope_tables(pos_d, inv_row_a, mult_a, ("cos", "sin")))

    inv_b = _inv_freq(MLA_ROPE)
    hb = MLA_ROPE // 2
    zeros_h = jnp.zeros((hb,), F32)
    ones_h = jnp.ones((hb,), F32)
    pad = jnp.zeros((LANES - MLA_ROPE,), F32)
    inv_row_b = jnp.concatenate([inv_b, inv_b, pad])[None, :]
    mult_b = jnp.stack([jnp.concatenate([ones_h, ones_h, pad]),
                        jnp.concatenate([zeros_h, ones_h, pad]),
                        jnp.concatenate([-ones_h, zeros_h, pad])])
    cos_b, sa_b, sb_b = _rope_tables(positions.reshape(tokens, 1).astype(F32), inv_row_b, mult_b,
                                     ("cos", "sin", "sin"))

    x2 = x.reshape(tokens, dm)
    xb = None
    qkv_w = 3 * DSA_WIDTH
    for layer in range(DEPTH):
        j = layer // N_MIXERS
        x3 = x2.reshape(batch, seq, dm)
        if layer % N_MIXERS == 0:
            w_in = dsa_w_in[j].astype(BF16)
            if xb is None:
                xbs = dict(zip(dils, _cast_perm(x3, dils)))
            else:
                rest = tuple(d for d in dils if d != 1)
                xbs = dict(zip(rest, _cast_perm(x3, rest)))
                xbs[1] = xb
            hs = []
            q_scale = DSA_HEAD_DIM ** -0.5 * math.log2(math.e)
            for g, d in enumerate(dils):
                cos, sin = tables_a[g]
                hs.append((_mm_rope(xbs[d], w_in, g * qkv_w, cos, sin, q_scale),
                           _mm(xbs[d], w_in, g * qkv_w + 2 * DSA_WIDTH, DSA_WIDTH)))
            z = _mm(xbs[1], w_in, DSA_GROUPS * qkv_w)
            y = _dsa_attention(hs, z, batch, seq)
            w_out = dsa_w_out[j].astype(BF16)
        else:
            if xb is None:
                xb = _cast_perm(x3, (1,))[0]
            w_in = mla_w_in[j]
            o2 = MLA_Q_RANK + MLA_KV_RANK
            o3 = o2 + MLA_ROPE
            w1 = jnp.pad(w_in[:, :o3], ((0, 0), (0, LANES - MLA_ROPE))).astype(BF16)
            wz = w_in[:, o3:].astype(BF16)
            wq = mla_w_uq[j].reshape(MLA_Q_RANK, MLA_HEADS, MLA_NOPE + MLA_ROPE)
            wqn = wq[:, :, :MLA_NOPE].reshape(MLA_Q_RANK, MLA_HEADS * MLA_NOPE).astype(BF16)
            wqp = jnp.pad(wq[:, :, MLA_NOPE:], ((0, 0), (0, 0), (0, LANES - MLA_ROPE))).reshape(
                MLA_Q_RANK, MLA_HEADS * LANES).astype(BF16)
            wkv = mla_w_ukv[j].reshape(MLA_KV_RANK, MLA_HEADS, MLA_NOPE + MLA_V)
            wkn = wkv[:, :, :MLA_NOPE].reshape(MLA_KV_RANK, MLA_HEADS * MLA_NOPE).astype(BF16)
            wvt = wkv[:, :, MLA_NOPE:].reshape(MLA_KV_RANK, MLA_HEADS * MLA_V).T.astype(BF16)
            q_scale = (MLA_NOPE + MLA_ROPE) ** -0.5 * math.log2(math.e)
            q, k, vt = _mla_proj(xb, w1, wqn, wqp, wkn, wvt, mla_q_norm[j][None, :], mla_kv_norm[j][None, :],
                                 cos_b, sa_b, sb_b, q_scale, batch, seq)
            z = _mm(xb, wz)
            y = _mla_flash(q, k, vt, z, batch, seq)
            w_out = mla_w_out[j].astype(BF16)
        x2, xb = _out_ln(y, w_out, x2, ln_g[layer][None, :], ln_b[layer][None, :])
    return x2.reshape(batch, seq, dm)
```

```python
import functools
import math

import jax
import jax.numpy as jnp
import numpy as np
from jax import lax
from jax.experimental import pallas as pl
from jax.experimental.pallas import tpu as pltpu

F32 = jnp.float32
BF16 = jnp.bfloat16

D_MODEL = 2048
DEPTH = 4
N_MIXERS = 2

DSA_PATTERNS = ((128, 1), (512, 4), (2048, 16))
DSA_GROUPS = len(DSA_PATTERNS)
DSA_HEADS = 16
DSA_HEAD_DIM = 128
DSA_WIDTH = DSA_HEADS * DSA_HEAD_DIM
DSA_BLOCK = 128

MLA_HEADS = 16
MLA_Q_RANK = 512
MLA_KV_RANK = 512
MLA_NOPE = 128
MLA_ROPE = 64
MLA_V = 128
MLA_WIDTH = MLA_HEADS * MLA_V

ROPE_THETA = 10000.0
RMS_EPS = 1e-6
LN_EPS = 1e-5
DEEPNORM_ALPHA = (2 * DEPTH) ** 0.25

LANES = 128
MIB = 1024 * 1024

DSA_TILE = 2048
DSA_HEADS_PER_STEP = 2
DSA_BLOCKS_PER_ITER = 8
MLA_TILE = 1024
MLA_HEADS_PER_STEP = 8
MLA_QUERY_CHUNKS = 4
MLA_VT_ROWS = MLA_V + 16
MLA_QK = 2 * LANES


def _params(semantics, vmem_mib):
    return pltpu.CompilerParams(dimension_semantics=semantics, vmem_limit_bytes=vmem_mib * MIB)


def _rope_table_kernel(pos_ref, inv_ref, mult_ref, *out_refs, kinds):
    ang = pos_ref[...] * inv_ref[...]
    c = jnp.cos(ang)
    s = jnp.sin(ang)
    for i, (kind, o_ref) in enumerate(zip(kinds, out_refs)):
        o_ref[...] = (c if kind == "cos" else s) * mult_ref[i:i + 1, :]


def _rope_tables(pos_f32, inv_row, mult_rows, kinds, tm=2048):
    n = pos_f32.shape[0]
    nk = len(kinds)
    return pl.pallas_call(
        functools.partial(_rope_table_kernel, kinds=kinds),
        grid=(n // tm,),
        in_specs=[pl.BlockSpec((tm, 1), lambda i: (i, 0)),
                  pl.BlockSpec((1, LANES), lambda i: (0, 0)),
                  pl.BlockSpec((nk, LANES), lambda i: (0, 0))],
        out_specs=[pl.BlockSpec((tm, LANES), lambda i: (i, 0))] * nk,
        out_shape=[jax.ShapeDtypeStruct((n, LANES), F32)] * nk,
        compiler_params=_params(("arbitrary",), 32),
        name="rope_tables",
    )(pos_f32, inv_row, mult_rows)


def _cast_perm_kernel(x_ref, *refs, dilations):
    o_refs, col_scr = refs[:-1], refs[-1]
    tm, dm = x_ref.shape[1], x_ref.shape[2]
    strided = any(d != 1 for d in dilations)
    for c in range(dm // LANES):
        cs = slice(c * LANES, (c + 1) * LANES)
        if strided:
            col_scr[c] = x_ref[0, :, cs]
        for d, o_ref in zip(dilations, o_refs):
            if d == 1:
                o_ref[0, 0, :, cs] = x_ref[0, :, cs].astype(BF16)
            else:
                for r in range(d):
                    o_ref[0, r, :, cs] = col_scr[c, pl.ds(r, tm // d, stride=d), :].astype(BF16)


def _cast_perm(x3, dilations, tm=1024):
    b, s, dm = x3.shape
    outs = pl.pallas_call(
        functools.partial(_cast_perm_kernel, dilations=dilations),
        grid=(b, s // tm),
        in_specs=[pl.BlockSpec((1, tm, dm), lambda bi, t: (bi, t, 0))],
        out_specs=[pl.BlockSpec((1, d, tm // d, dm), lambda bi, t: (bi, 0, t, 0)) for d in dilations],
        out_shape=[jax.ShapeDtypeStruct((b, d, s // d, dm), BF16) for d in dilations],
        scratch_shapes=[pltpu.VMEM((dm // LANES, tm, LANES), F32)],
        compiler_params=_params(("arbitrary",) * 2, 48),
        name="cast_perm",
    )(x3)
    return [o.reshape(b * s, dm) for o in outs]


def _mm_kernel(a_ref, w_ref, o_ref):
    o_ref[...] = jnp.dot(a_ref[...], w_ref[...], preferred_element_type=F32).astype(o_ref.dtype)


def _mm(a, w, col0=0, ncols=None, out_dtype=BF16, tm=2048, tn=1024):
    m, k = a.shape
    n = w.shape[1] - col0 if ncols is None else ncols
    j0 = col0 // tn
    return pl.pallas_call(
        _mm_kernel,
        grid=(m // tm, n // tn),
        in_specs=[pl.BlockSpec((tm, k), lambda i, j: (i, 0)),
                  pl.BlockSpec((k, tn), lambda i, j: (0, j + j0))],
        out_specs=pl.BlockSpec((tm, tn), lambda i, j: (i, j)),
        out_shape=jax.ShapeDtypeStruct((m, n), out_dtype),
        compiler_params=_params(("arbitrary", "arbitrary"), 56),
        name="matmul",
    )(a, w)


def _mm_rope_kernel(a_ref, w_ref, cos_ref, sin_ref, o_ref, *, n_q_blocks, q_scale):
    acc = jnp.dot(a_ref[...], w_ref[...], preferred_element_type=F32)
    scale = jnp.where(pl.program_id(1) < n_q_blocks, q_scale, 1.0).astype(F32)
    cos = cos_ref[...] * scale
    sin = sin_ref[...] * scale
    for c in range(acc.shape[1] // LANES):
        xs = acc[:, c * LANES:(c + 1) * LANES]
        o_ref[:, c * LANES:(c + 1) * LANES] = (
            xs * cos + pltpu.roll(xs, LANES // 2, 1) * sin).astype(o_ref.dtype)


def _mm_rope(a, w, col0, cos, sin, q_scale, tm=2048, tn=1024):
    m, k = a.shape
    n = 2 * DSA_WIDTH
    j0 = col0 // tn
    kern = functools.partial(_mm_rope_kernel, n_q_blocks=DSA_WIDTH // tn, q_scale=q_scale)
    return pl.pallas_call(
        kern,
        grid=(m // tm, n // tn),
        in_specs=[pl.BlockSpec((tm, k), lambda i, j: (i, 0)),
                  pl.BlockSpec((k, tn), lambda i, j: (0, j + j0)),
                  pl.BlockSpec((tm, LANES), lambda i, j: (i, 0)),
                  pl.BlockSpec((tm, LANES), lambda i, j: (i, 0))],
        out_specs=pl.BlockSpec((tm, tn), lambda i, j: (i, j)),
        out_shape=jax.ShapeDtypeStruct((m, n), BF16),
        compiler_params=_params(("arbitrary", "arbitrary"), 56),
        name="matmul_rope",
    )(a, w, cos, sin)


def _band_attention(qs, kwins, vwins, biases):
    ss = [lax.dot_general(q, kw, (((1,), (1,)), ((), ())), preferred_element_type=F32) + b
          for q, kw, b in zip(qs, kwins, biases)]
    ms = [jnp.max(s, axis=1, keepdims=True) for s in ss]
    ps = [jnp.exp2(s - m) for s, m in zip(ss, ms)]
    ls = [jnp.sum(p, axis=1, keepdims=True) for p in ps]
    os_ = [jnp.dot(p.astype(BF16), vw, preferred_element_type=F32) for p, vw in zip(ps, vwins)]
    return [(o * (1.0 / l), jnp.broadcast_to(m + jnp.log2(l), o.shape)) for o, m, l in zip(os_, ms, ls)]


def _dsa_attn_kernel(*refs, dilations, spans, heads, unroll):
    ng = len(dilations)
    qkv_refs = refs[:3 * ng]
    z_ref, o_ref = refs[3 * ng], refs[3 * ng + 1]
    win_refs = refs[3 * ng + 2:3 * ng + 2 + 2 * ng]
    bias_ref, o_scr, l_scr = refs[3 * ng + 2 + 2 * ng:]
    blk = DSA_BLOCK
    tile = z_ref.shape[1]
    t = pl.program_id(2)

    for g in range(ng):
        rows = tile // dilations[g]
        for src, dst in ((qkv_refs[3 * g + 1], win_refs[2 * g]), (qkv_refs[3 * g + 2], win_refs[2 * g + 1])):
            @pl.when(t == 0)
            def _(dst=dst):
                dst[:, 0:blk, :] = jnp.zeros((dst.shape[0], blk, dst.shape[2]), dst.dtype)

            @pl.when(t > 0)
            def _(dst=dst, rows=rows):
                dst[:, 0:blk, :] = dst[:, rows:rows + blk, :]

            dst[:, blk:blk + rows, :] = src[0]

    row = lax.broadcasted_iota(jnp.int32, (blk, 2 * blk), 0)
    col = lax.broadcasted_iota(jnp.int32, (blk, 2 * blk), 1)
    dist = row + blk - col
    for g in range(ng):
        band = (dist >= 0) & (dist <= spans[g])
        bias_ref[g, 0] = jnp.where(band, 0.0, -jnp.inf).astype(F32)
        bias_ref[g, 1] = jnp.where(band & (col >= blk), 0.0, -jnp.inf).astype(F32)

    n_items = tile // blk
    for g in range(ng):
        d = dilations[g]
        q_ref = qkv_refs[3 * g]
        kw_ref, vw_ref = win_refs[2 * g:2 * g + 2]
        nb = n_items // d

        def items(it, carry, g=g, d=d, q_ref=q_ref, kw_ref=kw_ref, vw_ref=vw_ref, nb=nb):
            qs, kwins, vwins, biases, dests = [], [], [], [], []
            for u in range(unroll):
                idx = it * unroll + u
                r = idx // nb
                bb = idx % nb
                r0 = pl.multiple_of(bb * blk, blk)
                first = jnp.logical_and(t == 0, bb == 0).astype(jnp.int32)
                start = r0 * d + r
                rows_idx = pl.ds(start, blk) if d == 1 else pl.ds(start, blk, stride=d)
                for hh in range(heads):
                    ln = slice(hh * LANES, (hh + 1) * LANES)
                    qs.append(q_ref[0, r, pl.ds(r0, blk), ln])
                    kwins.append(kw_ref[r, pl.ds(r0, 2 * blk), ln])
                    vwins.append(vw_ref[r, pl.ds(r0, 2 * blk), ln])
                    biases.append(bias_ref[g, first])
                    dests.append((hh, rows_idx))
            for (o, lse), (hh, rows_idx) in zip(_band_attention(qs, kwins, vwins, biases), dests):
                o_scr[hh, g, rows_idx, :] = o
                l_scr[hh, g, rows_idx, :] = lse
            return carry

        lax.fori_loop(0, n_items // unroll, items, 0)

    chunk = 256

    def finish(c, carry):
        r0 = pl.multiple_of(c * chunk, chunk)
        for hh in range(heads):
            ln = slice(hh * LANES, (hh + 1) * LANES)
            ls = [l_scr[hh, g, pl.ds(r0, chunk), :] for g in range(ng)]
            mx = functools.reduce(jnp.maximum, ls)
            es = [jnp.exp2(l - mx) for l in ls]
            den = functools.reduce(lambda a, b: a + b, es)
            num = functools.reduce(
                lambda a, b: a + b, [es[g] * o_scr[hh, g, pl.ds(r0, chunk), :] for g in range(ng)])
            zz = z_ref[0, pl.ds(r0, chunk), ln].astype(F32)
            y = (num * zz) / (den * (1.0 + jnp.exp(-zz)))
            o_ref[0, pl.ds(r0, chunk), ln] = y.astype(o_ref.dtype)
        return carry

    lax.fori_loop(0, tile // chunk, finish, 0)


def _dsa_attention(hs, z, batch, seq):
    heads = DSA_HEADS_PER_STEP
    wl = heads * LANES
    ncol = DSA_WIDTH // wl
    tile = DSA_TILE
    dil = tuple(d for _, d in DSA_PATTERNS)
    spans = tuple(w // d for w, d in DSA_PATTERNS)
    for d, sp in zip(dil, spans):
        assert sp <= DSA_BLOCK and tile % (d * DSA_BLOCK) == 0 and seq % tile == 0
    assert (tile // DSA_BLOCK) % DSA_BLOCKS_PER_ITER == 0

    in_specs, args, scratch = [], [], []
    for g, d in enumerate(dil):
        hqk, hv = hs[g]
        for kind in range(3):
            in_specs.append(pl.BlockSpec(
                (1, d, tile // d, wl), lambda b, h, t, kind=kind: (b, 0, t, (kind % 2) * ncol + h)))
            src = hv if kind == 2 else hqk
            args.append(src.reshape(batch, d, seq // d, src.shape[1]))
        scratch += [pltpu.VMEM((d, DSA_BLOCK + tile // d, wl), BF16)] * 2
    in_specs.append(pl.BlockSpec((1, tile, wl), lambda b, h, t: (b, t, h)))
    args.append(z.reshape(batch, seq, DSA_WIDTH))
    scratch.append(pltpu.VMEM((len(dil), 2, DSA_BLOCK, 2 * DSA_BLOCK), F32))
    scratch += [pltpu.VMEM((heads, len(dil), tile, LANES), F32)] * 2

    out = pl.pallas_call(
        functools.partial(_dsa_attn_kernel, dilations=dil, spans=spans, heads=heads,
                          unroll=DSA_BLOCKS_PER_ITER),
        grid=(batch, ncol, seq // tile),
        in_specs=in_specs,
        out_specs=pl.BlockSpec((1, tile, wl), lambda b, h, t: (b, t, h)),
        out_shape=jax.ShapeDtypeStruct((batch, seq, DSA_WIDTH), BF16),
        scratch_shapes=scratch,
        compiler_params=_params(("arbitrary",) * 3, 56),
        name="dsa_attention",
    )(*args)
    return out.reshape(batch * seq, DSA_WIDTH)


def _out_ln_kernel(y_ref, w_ref, x_ref, g_ref, b_ref, o_ref, ob_ref):
    r = DEEPNORM_ALPHA * x_ref[...] + jnp.dot(y_ref[...], w_ref[...], preferred_element_type=F32)
    mu = jnp.mean(r, axis=-1, keepdims=True)
    xc = r - mu
    var = jnp.mean(xc * xc, axis=-1, keepdims=True)
    out = xc * lax.rsqrt(var + LN_EPS) * g_ref[...] + b_ref[...]
    o_ref[...] = out
    ob_ref[...] = out.astype(BF16)


def _out_ln(y, w, x, g, b, tm=512):
    m, k = y.shape
    n = w.shape[1]
    row = lambda i: (i, 0)
    fixed = lambda i: (0, 0)
    return pl.pallas_call(
        _out_ln_kernel,
        grid=(m // tm,),
        in_specs=[pl.BlockSpec((tm, k), row), pl.BlockSpec((k, n), fixed, pipeline_mode=pl.Buffered(1)),
                  pl.BlockSpec((tm, n), row), pl.BlockSpec((1, n), fixed), pl.BlockSpec((1, n), fixed)],
        out_specs=[pl.BlockSpec((tm, n), row), pl.BlockSpec((tm, n), row)],
        out_shape=[jax.ShapeDtypeStruct((m, n), F32), jax.ShapeDtypeStruct((m, n), BF16)],
        compiler_params=_params(("arbitrary",), 56),
        name="out_proj_layernorm",
    )(y, w, x, g, b)


def _mla_proj_kernel(x_ref, w1_ref, wqn_ref, wqp_ref, wkn_ref, wvt_ref, qg_ref, kvg_ref,
                     cos_ref, sa_ref, sb_ref, q_ref, k_ref, vt_ref, *, scale):
    h1 = jnp.dot(x_ref[0], w1_ref[...], preferred_element_type=F32)
    cq = h1[:, :MLA_Q_RANK]
    ckv = h1[:, MLA_Q_RANK:MLA_Q_RANK + MLA_KV_RANK]
    kpe = h1[:, MLA_Q_RANK + MLA_KV_RANK:]

    def rms(c, g):
        y = c * lax.rsqrt(jnp.mean(c * c, axis=-1, keepdims=True) + RMS_EPS)
        return (y * g).astype(BF16)

    cos, sa, sb = cos_ref[0], sa_ref[0], sb_ref[0]

    def rope(xs):
        return xs * cos + pltpu.roll(xs, MLA_ROPE // 2, 1) * sa + pltpu.roll(xs, LANES - MLA_ROPE // 2, 1) * sb

    cqn = rms(cq, qg_ref[...])
    ckvn = rms(ckv, kvg_ref[...])
    kpe_r = rope(kpe).astype(BF16)
    qn = jnp.dot(cqn, wqn_ref[...], preferred_element_type=F32)
    qp = jnp.dot(cqn, wqp_ref[...], preferred_element_type=F32)
    kn = jnp.dot(ckvn, wkn_ref[...], preferred_element_type=F32)
    vt = lax.dot_general(wvt_ref[...], ckvn, (((1,), (1,)), ((), ())), preferred_element_type=F32)
    extra = MLA_VT_ROWS - MLA_V
    ones_row = (lax.broadcasted_iota(jnp.int32, (extra, vt.shape[1]), 0) == 0).astype(BF16)
    for h in range(MLA_HEADS):
        vt_ref[0, h, MLA_V:MLA_VT_ROWS, :] = ones_row
        hs = slice(h * LANES, (h + 1) * LANES)
        q_ref[0, h, :, 0:LANES] = (qn[:, hs] * scale).astype(BF16)
        q_ref[0, h, :, LANES:MLA_QK] = (rope(qp[:, hs]) * scale).astype(BF16)
        k_ref[0, h, :, 0:LANES] = kn[:, hs].astype(BF16)
        k_ref[0, h, :, LANES:MLA_QK] = kpe_r
        vt_ref[0, h, 0:MLA_V, :] = vt[h * MLA_V:(h + 1) * MLA_V, :].astype(BF16)


def _mla_proj(xb, w1, wqn, wqp, wkn, wvt, qg, kvg, cos, sa, sb, scale, batch, seq, tm=256):
    dm = xb.shape[1]
    row = lambda b, i: (b, i, 0)
    fixed = lambda b, i: (0, 0)
    wspec = lambda w: pl.BlockSpec(w.shape, fixed, pipeline_mode=pl.Buffered(1))
    tab = lambda a: a.reshape(batch, seq, LANES)
    return pl.pallas_call(
        functools.partial(_mla_proj_kernel, scale=scale),
        grid=(batch, seq // tm),
        in_specs=[pl.BlockSpec((1, tm, dm), row), wspec(w1), wspec(wqn), wspec(wqp), wspec(wkn), wspec(wvt),
                  wspec(qg), wspec(kvg)] + [pl.BlockSpec((1, tm, LANES), row)] * 3,
        out_specs=[pl.BlockSpec((1, MLA_HEADS, tm, MLA_QK), lambda b, i: (b, 0, i, 0)),
                   pl.BlockSpec((1, MLA_HEADS, tm, MLA_QK), lambda b, i: (b, 0, i, 0)),
                   pl.BlockSpec((1, MLA_HEADS, MLA_VT_ROWS, tm), lambda b, i: (b, 0, 0, i))],
        out_shape=[jax.ShapeDtypeStruct((batch, MLA_HEADS, seq, MLA_QK), BF16),
                   jax.ShapeDtypeStruct((batch, MLA_HEADS, seq, MLA_QK), BF16),
                   jax.ShapeDtypeStruct((batch, MLA_HEADS, MLA_VT_ROWS, seq), BF16)],
        compiler_params=_params(("arbitrary",) * 2, 56),
        name="mla_projections",
    )(xb.reshape(batch, seq, dm), w1, wqn, wqp, wkn, wvt, qg, kvg, tab(cos), tab(sa), tab(sb))


def _mla_flash_kernel(qi_ref, ki_ref, q_ref, k_ref, vt_ref, z_ref, o_ref, s_buf, smax_buf, m_scr, acc_scr, *,
                      heads):
    t = pl.program_id(2)
    qi = qi_ref[t]
    ki = ki_ref[t]
    tq = q_ref.shape[2]
    tk = k_ref.shape[2]

    @pl.when(ki == 0)
    def _():
        m_scr[...] = jnp.full(m_scr.shape, -jnp.inf, F32)
        acc_scr[...] = jnp.zeros(acc_scr.shape, F32)

    qc = tq // MLA_QUERY_CHUNKS

    def scores(h, c, masked):
        cs = slice(c * qc, (c + 1) * qc)
        s = lax.dot_general(k_ref[0, h], q_ref[0, h, cs, :], (((1,), (1,)), ((), ())),
                            preferred_element_type=F32)
        if masked:
            key = lax.broadcasted_iota(jnp.int32, (tk, qc), 0)
            qry = lax.broadcasted_iota(jnp.int32, (tk, qc), 1) + c * qc
            s = jnp.where(qry >= key, s, -jnp.inf)
        s_buf[h % 2, :, cs] = s
        smax_buf[h % 2, :, cs] = jnp.max(s, axis=0, keepdims=True)

    def softmax_pv(h, c):
        cs = slice(c * qc, (c + 1) * qc)
        m_prev = m_scr[h, :, cs]
        m_new = jnp.maximum(m_prev, smax_buf[h % 2, :, cs])
        alpha = jnp.exp2(m_prev - m_new)
        p = jnp.exp2(s_buf[h % 2, :, cs] - m_new)
        m_scr[h, :, cs] = m_new
        acc_scr[h, :, cs] = alpha * acc_scr[h, :, cs] + jnp.dot(
            vt_ref[0, h], p.astype(BF16), preferred_element_type=F32)

    def update(masked):
        for c in range(MLA_QUERY_CHUNKS):
            scores(0, c, masked)
        for h in range(heads):
            for c in range(MLA_QUERY_CHUNKS):
                softmax_pv(h, c)
                if h + 1 < heads:
                    scores(h + 1, c, masked)

    @pl.when(ki < qi)
    def _():
        update(False)

    @pl.when(ki == qi)
    def _():
        update(True)
        for h in range(heads):
            ln = slice(h * MLA_V, (h + 1) * MLA_V)
            zz = z_ref[0, :, ln].astype(F32)
            acc = acc_scr[h]
            o = (acc[:MLA_V] * (1.0 / acc[MLA_V:MLA_V + 1])).T
            o_ref[0, :, ln] = (o * (zz / (1.0 + jnp.exp(-zz)))).astype(o_ref.dtype)


def _mla_flash(q, k, vt, z, batch, seq):
    heads = MLA_HEADS_PER_STEP
    tile = MLA_TILE
    nt = seq // tile
    qi = np.concatenate([np.full(i + 1, i, np.int32) for i in range(nt)])
    ki = np.concatenate([np.arange(i + 1, dtype=np.int32) for i in range(nt)])
    v_w = heads * MLA_V
    qmap = lambda b, h, t, qi_r, ki_r: (b, h, qi_r[t], 0)
    kmap = lambda b, h, t, qi_r, ki_r: (b, h, ki_r[t], 0)
    vmap = lambda b, h, t, qi_r, ki_r: (b, h, 0, ki_r[t])
    zmap = lambda b, h, t, qi_r, ki_r: (b, qi_r[t], h)
    grid_spec = pltpu.PrefetchScalarGridSpec(
        num_scalar_prefetch=2,
        grid=(batch, MLA_HEADS // heads, len(qi)),
        in_specs=[pl.BlockSpec((1, heads, tile, MLA_QK), qmap), pl.BlockSpec((1, heads, tile, MLA_QK), kmap),
                  pl.BlockSpec((1, heads, MLA_VT_ROWS, tile), vmap), pl.BlockSpec((1, tile, v_w), zmap)],
        out_specs=pl.BlockSpec((1, tile, v_w), zmap),
        scratch_shapes=[pltpu.VMEM((2, tile, tile), F32), pltpu.VMEM((2, 1, tile), F32),
                        pltpu.VMEM((heads, 1, tile), F32), pltpu.VMEM((heads, MLA_VT_ROWS, tile), F32)],
    )
    out = pl.pallas_call(
        functools.partial(_mla_flash_kernel, heads=heads),
        grid_spec=grid_spec,
        out_shape=jax.ShapeDtypeStruct((batch, seq, MLA_WIDTH), BF16),
        compiler_params=_params(("arbitrary",) * 3, 56),
        name="mla_flash_attention",
    )(jnp.asarray(qi), jnp.asarray(ki), q, k, vt, z.reshape(batch, seq, MLA_WIDTH))
    return out.reshape(batch * seq, MLA_WIDTH)


def _inv_freq(dim):
    return 1.0 / (ROPE_THETA ** (jnp.arange(0, dim, 2, dtype=F32) / dim))


def kernel(x, positions, dsa_w_in, dsa_w_out, mla_w_in, mla_q_norm, mla_w_uq, mla_kv_norm, mla_w_ukv,
           mla_w_out, ln_g, ln_b):
    batch, seq, dm = x.shape
    tokens = batch * seq
    half = DSA_HEAD_DIM // 2
    dils = tuple(d for _, d in DSA_PATTERNS)

    inv_a = _inv_freq(DSA_HEAD_DIM)
    inv_row_a = jnp.concatenate([inv_a, inv_a])[None, :]
    mult_a = jnp.stack([jnp.ones((LANES,), F32),
                        jnp.concatenate([-jnp.ones((half,), F32), jnp.ones((half,), F32)])])
    tables_a = []
    for d in dils:
        pos_d = positions.reshape(batch, seq // d, d).transpose(0, 2, 1).reshape(tokens, 1).astype(F32)
        tables_a.append(_rope_tables(pos_d, inv_row_a, mult_a, ("cos", "sin")))

    inv_b = _inv_freq(MLA_ROPE)
    hb = MLA_ROPE // 2
    zeros_h = jnp.zeros((hb,), F32)
    ones_h = jnp.ones((hb,), F32)
    pad = jnp.zeros((LANES - MLA_ROPE,), F32)
    inv_row_b = jnp.concatenate([inv_b, inv_b, pad])[None, :]
    mult_b = jnp.stack([jnp.concatenate([ones_h, ones_h, pad]),
                        jnp.concatenate([zeros_h, ones_h, pad]),
                        jnp.concatenate([-ones_h, zeros_h, pad])])
    cos_b, sa_b, sb_b = _rope_tables(positions.reshape(tokens, 1).astype(F32), inv_row_b, mult_b,
                                     ("cos", "sin", "sin"))

    x2 = x.reshape(tokens, dm)
    xb = None
    qkv_w = 3 * DSA_WIDTH
    for layer in range(DEPTH):
        j = layer // N_MIXERS
        x3 = x2.reshape(batch, seq, dm)
        if layer % N_MIXERS == 0:
            w_in = dsa_w_in[j].astype(BF16)
            if xb is None:
                xbs = dict(zip(dils, _cast_perm(x3, dils)))
            else:
                rest = tuple(d for d in dils if d != 1)
                xbs = dict(zip(rest, _cast_perm(x3, rest)))
                xbs[1] = xb
            hs = []
            q_scale = DSA_HEAD_DIM ** -0.5 * math.log2(math.e)
            for g, d in enumerate(dils):
                cos, sin = tables_a[g]
                hs.append((_mm_rope(xbs[d], w_in, g * qkv_w, cos, sin, q_scale),
                           _mm(xbs[d], w_in, g * qkv_w + 2 * DSA_WIDTH, DSA_WIDTH)))
            z = _mm(xbs[1], w_in, DSA_GROUPS * qkv_w)
            y = _dsa_attention(hs, z, batch, seq)
            w_out = dsa_w_out[j].astype(BF16)
        else:
            if xb is None:
                xb = _cast_perm(x3, (1,))[0]
            w_in = mla_w_in[j]
            o2 = MLA_Q_RANK + MLA_KV_RANK
            o3 = o2 + MLA_ROPE
            w1 = jnp.pad(w_in[:, :o3], ((0, 0), (0, LANES - MLA_ROPE))).astype(BF16)
            wz = w_in[:, o3:].astype(BF16)
            wq = mla_w_uq[j].reshape(MLA_Q_RANK, MLA_HEADS, MLA_NOPE + MLA_ROPE)
            wqn = wq[:, :, :MLA_NOPE].reshape(MLA_Q_RANK, MLA_HEADS * MLA_NOPE).astype(BF16)
            wqp = jnp.pad(wq[:, :, MLA_NOPE:], ((0, 0), (0, 0), (0, LANES - MLA_ROPE))).reshape(
                MLA_Q_RANK, MLA_HEADS * LANES).astype(BF16)
            wkv = mla_w_ukv[j].reshape(MLA_KV_RANK, MLA_HEADS, MLA_NOPE + MLA_V)
            wkn = wkv[:, :, :MLA_NOPE].reshape(MLA_KV_RANK, MLA_HEADS * MLA_NOPE).astype(BF16)
            wvt = wkv[:, :, MLA_NOPE:].reshape(MLA_KV_RANK, MLA_HEADS * MLA_V).T.astype(BF16)
            q_scale = (MLA_NOPE + MLA_ROPE) ** -0.5 * math.log2(math.e)
            q, k, vt = _mla_proj(xb, w1, wqn, wqp, wkn, wvt, mla_q_norm[j][None, :], mla_kv_norm[j][None, :],
                                 cos_b, sa_b, sb_b, q_scale, batch, seq)
            z = _mm(xb, wz)
            y = _mla_flash(q, k, vt, z, batch, seq)
            w_out = mla_w_out[j].astype(BF16)
        x2, xb = _out_ln(y, w_out, x2, ln_g[layer][None, :], ln_b[layer][None, :])
    return x2.reshape(batch, seq, dm)
```

```python
import functools
import math

import jax
import jax.numpy as jnp
import numpy as np
from jax import lax
from jax.experimental import pallas as pl
from jax.experimental.pallas import tpu as pltpu

F32 = jnp.float32
BF16 = jnp.bfloat16

D_MODEL = 2048
DEPTH = 4
N_MIXERS = 2

DSA_PATTERNS = ((128, 1), (512, 4), (2048, 16))
DSA_GROUPS = len(DSA_PATTERNS)
DSA_HEADS = 16
DSA_HEAD_DIM = 128
DSA_WIDTH = DSA_HEADS * DSA_HEAD_DIM
DSA_BLOCK = 128

MLA_HEADS = 16
MLA_Q_RANK = 512
MLA_KV_RANK = 512
MLA_NOPE = 128
MLA_ROPE = 64
MLA_V = 128
MLA_WIDTH = MLA_HEADS * MLA_V

ROPE_THETA = 10000.0
RMS_EPS = 1e-6
LN_EPS = 1e-5
DEEPNORM_ALPHA = (2 * DEPTH) ** 0.25

LANES = 128
MIB = 1024 * 1024

DSA_TILE = 2048
DSA_HEADS_PER_STEP = 2
DSA_BLOCKS_PER_ITER = 8
MLA_TILE = 1024
MLA_HEADS_PER_STEP = 8
MLA_QUERY_CHUNKS = 4
MLA_VT_ROWS = MLA_V + 16
MLA_QK = 2 * LANES


def _params(semantics, vmem_mib):
    return pltpu.CompilerParams(dimension_semantics=semantics, vmem_limit_bytes=vmem_mib * MIB)


def _rope_table_kernel(pos_ref, inv_ref, mult_ref, *out_refs, kinds):
    ang = pos_ref[...] * inv_ref[...]
    c = jnp.cos(ang)
    s = jnp.sin(ang)
    for i, (kind, o_ref) in enumerate(zip(kinds, out_refs)):
        o_ref[...] = (c if kind == "cos" else s) * mult_ref[i:i + 1, :]


def _rope_tables(pos_f32, inv_row, mult_rows, kinds, tm=2048):
    n = pos_f32.shape[0]
    nk = len(kinds)
    return pl.pallas_call(
        functools.partial(_rope_table_kernel, kinds=kinds),
        grid=(n // tm,),
        in_specs=[pl.BlockSpec((tm, 1), lambda i: (i, 0)),
                  pl.BlockSpec((1, LANES), lambda i: (0, 0)),
                  pl.BlockSpec((nk, LANES), lambda i: (0, 0))],
        out_specs=[pl.BlockSpec((tm, LANES), lambda i: (i, 0))] * nk,
        out_shape=[jax.ShapeDtypeStruct((n, LANES), F32)] * nk,
        compiler_params=_params(("arbitrary",), 32),
        name="rope_tables",
    )(pos_f32, inv_row, mult_rows)


def _table_perm_kernel(*refs, dilations, ntab):
    tabs, outs = refs[:ntab], refs[ntab:]
    tm = tabs[0].shape[1]
    for i, d in enumerate(dilations):
        for j, tab in enumerate(tabs):
            for r in range(d):
                outs[i * ntab + j][0, r] = tab[0, pl.ds(r, tm // d, stride=d), :]


def _table_perm(tabs, dilations, batch, seq, tm=2048):
    ntab = len(tabs)
    outs = pl.pallas_call(
        functools.partial(_table_perm_kernel, dilations=dilations, ntab=ntab),
        grid=(batch, seq // tm),
        in_specs=[pl.BlockSpec((1, tm, LANES), lambda b, t: (b, t, 0))] * ntab,
        out_specs=[pl.BlockSpec((1, d, tm // d, LANES), lambda b, t: (b, 0, t, 0))
                   for d in dilations for _ in range(ntab)],
        out_shape=[jax.ShapeDtypeStruct((batch, d, seq // d, LANES), F32) for d in dilations for _ in range(ntab)],
        compiler_params=_params(("arbitrary",) * 2, 32),
        name="rope_table_perm",
    )(*[t.reshape(batch, seq, LANES) for t in tabs])
    outs = [o.reshape(batch * seq, LANES) for o in outs]
    return [tuple(outs[i * ntab:(i + 1) * ntab]) for i in range(len(dilations))]


def _cast_perm_kernel(x_ref, *refs, dilations):
    o_refs, col_scr = refs[:-1], refs[-1]
    tm, dm = x_ref.shape[1], x_ref.shape[2]
    strided = any(d != 1 for d in dilations)
    for c in range(dm // LANES):
        cs = slice(c * LANES, (c + 1) * LANES)
        if strided:
            col_scr[c] = x_ref[0, :, cs]
        for d, o_ref in zip(dilations, o_refs):
            if d == 1:
                o_ref[0, 0, :, cs] = x_ref[0, :, cs].astype(BF16)
            else:
                for r in range(d):
                    o_ref[0, r, :, cs] = col_scr[c, pl.ds(r, tm // d, stride=d), :].astype(BF16)


def _cast_perm(x3, dilations, tm=1024):
    b, s, dm = x3.shape
    outs = pl.pallas_call(
        functools.partial(_cast_perm_kernel, dilations=dilations),
        grid=(b, s // tm),
        in_specs=[pl.BlockSpec((1, tm, dm), lambda bi, t: (bi, t, 0))],
        out_specs=[pl.BlockSpec((1, d, tm // d, dm), lambda bi, t: (bi, 0, t, 0)) for d in dilations],
        out_shape=[jax.ShapeDtypeStruct((b, d, s // d, dm), BF16) for d in dilations],
        scratch_shapes=[pltpu.VMEM((dm // LANES, tm, LANES), F32)],
        compiler_params=_params(("arbitrary",) * 2, 48),
        name="cast_perm",
    )(x3)
    return [o.reshape(b * s, dm) for o in outs]


def _mm_kernel(a_ref, w_ref, o_ref):
    o_ref[...] = jnp.dot(a_ref[...], w_ref[...], preferred_element_type=F32).astype(o_ref.dtype)


def _mm(a, w, col0=0, ncols=None, out_dtype=BF16, tm=2048, tn=1024):
    m, k = a.shape
    n = w.shape[1] - col0 if ncols is None else ncols
    j0 = col0 // tn
    return pl.pallas_call(
        _mm_kernel,
        grid=(m // tm, n // tn),
        in_specs=[pl.BlockSpec((tm, k), lambda i, j: (i, 0)),
                  pl.BlockSpec((k, tn), lambda i, j: (0, j + j0))],
        out_specs=pl.BlockSpec((tm, tn), lambda i, j: (i, j)),
        out_shape=jax.ShapeDtypeStruct((m, n), out_dtype),
        compiler_params=_params(("arbitrary", "arbitrary"), 56),
        name="matmul",
    )(a, w)


def _mm_rope_kernel(a_ref, w_ref, cos_ref, sin_ref, o_ref, *, n_q_blocks, q_scale):
    acc = jnp.dot(a_ref[...], w_ref[...], preferred_element_type=F32)
    scale = jnp.where(pl.program_id(1) < n_q_blocks, q_scale, 1.0).astype(F32)
    cos = cos_ref[...] * scale
    sin = sin_ref[...] * scale
    for c in range(acc.shape[1] // LANES):
        xs = acc[:, c * LANES:(c + 1) * LANES]
        o_ref[:, c * LANES:(c + 1) * LANES] = (
            xs * cos + pltpu.roll(xs, LANES // 2, 1) * sin).astype(o_ref.dtype)


def _mm_rope(a, w, col0, cos, sin, q_scale, tm=2048, tn=1024):
    m, k = a.shape
    n = 2 * DSA_WIDTH
    j0 = col0 // tn
    kern = functools.partial(_mm_rope_kernel, n_q_blocks=DSA_WIDTH // tn, q_scale=q_scale)
    return pl.pallas_call(
        kern,
        grid=(m // tm, n // tn),
        in_specs=[pl.BlockSpec((tm, k), lambda i, j: (i, 0)),
                  pl.BlockSpec((k, tn), lambda i, j: (0, j + j0)),
                  pl.BlockSpec((tm, LANES), lambda i, j: (i, 0)),
                  pl.BlockSpec((tm, LANES), lambda i, j: (i, 0))],
        out_specs=pl.BlockSpec((tm, tn), lambda i, j: (i, j)),
        out_shape=jax.ShapeDtypeStruct((m, n), BF16),
        compiler_params=_params(("arbitrary", "arbitrary"), 56),
        name="matmul_rope",
    )(a, w, cos, sin)


def _band_attention(qs, kwins, vwins, biases):
    ss = [lax.dot_general(q, kw, (((1,), (1,)), ((), ())), preferred_element_type=F32) + b
          for q, kw, b in zip(qs, kwins, biases)]
    ms = [jnp.max(s, axis=1, keepdims=True) for s in ss]
    ps = [jnp.exp2(s - m) for s, m in zip(ss, ms)]
    ls = [jnp.sum(p, axis=1, keepdims=True) for p in ps]
    os_ = [jnp.dot(p.astype(BF16), vw, preferred_element_type=F32) for p, vw in zip(ps, vwins)]
    return [(o * (1.0 / l), jnp.broadcast_to(m + jnp.log2(l), o.shape)) for o, m, l in zip(os_, ms, ls)]


def _dsa_attn_kernel(*refs, dilations, spans, heads, unroll):
    ng = len(dilations)
    qkv_refs = refs[:3 * ng]
    z_ref, o_ref = refs[3 * ng], refs[3 * ng + 1]
    win_refs = refs[3 * ng + 2:3 * ng + 2 + 2 * ng]
    bias_ref, o_scr, l_scr = refs[3 * ng + 2 + 2 * ng:]
    blk = DSA_BLOCK
    tile = z_ref.shape[1]
    t = pl.program_id(2)

    for g in range(ng):
        rows = tile // dilations[g]
        for src, dst in ((qkv_refs[3 * g + 1], win_refs[2 * g]), (qkv_refs[3 * g + 2], win_refs[2 * g + 1])):
            @pl.when(t == 0)
            def _(dst=dst):
                dst[:, 0:blk, :] = jnp.zeros((dst.shape[0], blk, dst.shape[2]), dst.dtype)

            @pl.when(t > 0)
            def _(dst=dst, rows=rows):
                dst[:, 0:blk, :] = dst[:, rows:rows + blk, :]

            dst[:, blk:blk + rows, :] = src[0]

    row = lax.broadcasted_iota(jnp.int32, (blk, 2 * blk), 0)
    col = lax.broadcasted_iota(jnp.int32, (blk, 2 * blk), 1)
    dist = row + blk - col
    for g in range(ng):
        band = (dist >= 0) & (dist <= spans[g])
        bias_ref[g, 0] = jnp.where(band, 0.0, -jnp.inf).astype(F32)
        bias_ref[g, 1] = jnp.where(band & (col >= blk), 0.0, -jnp.inf).astype(F32)

    n_items = tile // blk
    for g in range(ng):
        d = dilations[g]
        q_ref = qkv_refs[3 * g]
        kw_ref, vw_ref = win_refs[2 * g:2 * g + 2]
        nb = n_items // d

        def items(it, carry, g=g, d=d, q_ref=q_ref, kw_ref=kw_ref, vw_ref=vw_ref, nb=nb):
            qs, kwins, vwins, biases, dests = [], [], [], [], []
            for u in range(unroll):
                idx = it * unroll + u
                r = idx // nb
                bb = idx % nb
                r0 = pl.multiple_of(bb * blk, blk)
                first = jnp.logical_and(t == 0, bb == 0).astype(jnp.int32)
                start = r0 * d + r
                rows_idx = pl.ds(start, blk) if d == 1 else pl.ds(start, blk, stride=d)
                for hh in range(heads):
                    ln = slice(hh * LANES, (hh + 1) * LANES)
                    qs.append(q_ref[0, r, pl.ds(r0, blk), ln])
                    kwins.append(kw_ref[r, pl.ds(r0, 2 * blk), ln])
                    vwins.append(vw_ref[r, pl.ds(r0, 2 * blk), ln])
                    biases.append(bias_ref[g, first])
                    dests.append((hh, rows_idx))
            for (o, lse), (hh, rows_idx) in zip(_band_attention(qs, kwins, vwins, biases), dests):
                o_scr[hh, g, rows_idx, :] = o
                l_scr[hh, g, rows_idx, :] = lse
            return carry

        lax.fori_loop(0, n_items // unroll, items, 0)

    chunk = 256

    def finish(c, carry):
        r0 = pl.multiple_of(c * chunk, chunk)
        for hh in range(heads):
            ln = slice(hh * LANES, (hh + 1) * LANES)
            ls = [l_scr[hh, g, pl.ds(r0, chunk), :] for g in range(ng)]
            mx = functools.reduce(jnp.maximum, ls)
            es = [jnp.exp2(l - mx) for l in ls]
            den = functools.reduce(lambda a, b: a + b, es)
            num = functools.reduce(
                lambda a, b: a + b, [es[g] * o_scr[hh, g, pl.ds(r0, chunk), :] for g in range(ng)])
            zz = z_ref[0, pl.ds(r0, chunk), ln].astype(F32)
            y = (num * zz) / (den * (1.0 + jnp.exp(-zz)))
            o_ref[0, pl.ds(r0, chunk), ln] = y.astype(o_ref.dtype)
        return carry

    lax.fori_loop(0, tile // chunk, finish, 0)


def _dsa_attention(hs, z, batch, seq):
    heads = DSA_HEADS_PER_STEP
    wl = heads * LANES
    ncol = DSA_WIDTH // wl
    tile = DSA_TILE
    dil = tuple(d for _, d in DSA_PATTERNS)
    spans = tuple(w // d for w, d in DSA_PATTERNS)
    for d, sp in zip(dil, spans):
        assert sp <= DSA_BLOCK and tile % (d * DSA_BLOCK) == 0 and seq % tile == 0
    assert (tile // DSA_BLOCK) % DSA_BLOCKS_PER_ITER == 0

    in_specs, args, scratch = [], [], []
    for g, d in enumerate(dil):
        hqk, hv = hs[g]
        for kind in range(3):
            in_specs.append(pl.BlockSpec(
                (1, d, tile // d, wl), lambda b, h, t, kind=kind: (b, 0, t, (kind % 2) * ncol + h)))
            src = hv if kind == 2 else hqk
            args.append(src.reshape(batch, d, seq // d, src.shape[1]))
        scratch += [pltpu.VMEM((d, DSA_BLOCK + tile // d, wl), BF16)] * 2
    in_specs.append(pl.BlockSpec((1, tile, wl), lambda b, h, t: (b, t, h)))
    args.append(z.reshape(batch, seq, DSA_WIDTH))
    scratch.append(pltpu.VMEM((len(dil), 2, DSA_BLOCK, 2 * DSA_BLOCK), F32))
    scratch += [pltpu.VMEM((heads, len(dil), tile, LANES), F32)] * 2

    out = pl.pallas_call(
        functools.partial(_dsa_attn_kernel, dilations=dil, spans=spans, heads=heads,
                          unroll=DSA_BLOCKS_PER_ITER),
        grid=(batch, ncol, seq // tile),
        in_specs=in_specs,
        out_specs=pl.BlockSpec((1, tile, wl), lambda b, h, t: (b, t, h)),
        out_shape=jax.ShapeDtypeStruct((batch, seq, DSA_WIDTH), BF16),
        scratch_shapes=scratch,
        compiler_params=_params(("arbitrary",) * 3, 56),
        name="dsa_attention",
    )(*args)
    return out.reshape(batch * seq, DSA_WIDTH)


def _out_ln_kernel(y_ref, w_ref, x_ref, g_ref, b_ref, o_ref, ob_ref):
    r = DEEPNORM_ALPHA * x_ref[...] + jnp.dot(y_ref[...], w_ref[...], preferred_element_type=F32)
    mu = jnp.mean(r, axis=-1, keepdims=True)
    xc = r - mu
    var = jnp.mean(xc * xc, axis=-1, keepdims=True)
    out = xc * lax.rsqrt(var + LN_EPS) * g_ref[...] + b_ref[...]
    o_ref[...] = out
    ob_ref[...] = out.astype(BF16)


def _out_ln(y, w, x, g, b, tm=512):
    m, k = y.shape
    n = w.shape[1]
    row = lambda i: (i, 0)
    fixed = lambda i: (0, 0)
    return pl.pallas_call(
        _out_ln_kernel,
        grid=(m // tm,),
        in_specs=[pl.BlockSpec((tm, k), row), pl.BlockSpec((k, n), fixed, pipeline_mode=pl.Buffered(1)),
                  pl.BlockSpec((tm, n), row), pl.BlockSpec((1, n), fixed), pl.BlockSpec((1, n), fixed)],
        out_specs=[pl.BlockSpec((tm, n), row), pl.BlockSpec((tm, n), row)],
        out_shape=[jax.ShapeDtypeStruct((m, n), F32), jax.ShapeDtypeStruct((m, n), BF16)],
        compiler_params=_params(("arbitrary",), 56),
        name="out_proj_layernorm",
    )(y, w, x, g, b)


def _mla_proj_kernel(x_ref, w1_ref, wqn_ref, wqp_ref, wkn_ref, wvt_ref, qg_ref, kvg_ref,
                     cos_ref, sa_ref, sb_ref, q_ref, k_ref, vt_ref, *, scale):
    h1 = jnp.dot(x_ref[0], w1_ref[...], preferred_element_type=F32)
    cq = h1[:, :MLA_Q_RANK]
    ckv = h1[:, MLA_Q_RANK:MLA_Q_RANK + MLA_KV_RANK]
    kpe = h1[:, MLA_Q_RANK + MLA_KV_RANK:]

    def rms(c, g):
        y = c * lax.rsqrt(jnp.mean(c * c, axis=-1, keepdims=True) + RMS_EPS)
        return (y * g).astype(BF16)

    cos, sa, sb = cos_ref[0], sa_ref[0], sb_ref[0]

    def rope(xs):
        return xs * cos + pltpu.roll(xs, MLA_ROPE // 2, 1) * sa + pltpu.roll(xs, LANES - MLA_ROPE // 2, 1) * sb

    cqn = rms(cq, qg_ref[...])
    ckvn = rms(ckv, kvg_ref[...])
    kpe_r = rope(kpe).astype(BF16)
    qn = jnp.dot(cqn, wqn_ref[...], preferred_element_type=F32)
    qp = jnp.dot(cqn, wqp_ref[...], preferred_element_type=F32)
    kn = jnp.dot(ckvn, wkn_ref[...], preferred_element_type=F32)
    vt = lax.dot_general(wvt_ref[...], ckvn, (((1,), (1,)), ((), ())), preferred_element_type=F32)
    extra = MLA_VT_ROWS - MLA_V
    ones_row = (lax.broadcasted_iota(jnp.int32, (extra, vt.shape[1]), 0) == 0).astype(BF16)
    for h in range(MLA_HEADS):
        vt_ref[0, h, MLA_V:MLA_VT_ROWS, :] = ones_row
        hs = slice(h * LANES, (h + 1) * LANES)
        q_ref[0, h, :, 0:LANES] = (qn[:, hs] * scale).astype(BF16)
        q_ref[0, h, :, LANES:MLA_QK] = (rope(qp[:, hs]) * scale).astype(BF16)
        k_ref[0, h, :, 0:LANES] = kn[:, hs].astype(BF16)
        k_ref[0, h, :, LANES:MLA_QK] = kpe_r
        vt_ref[0, h, 0:MLA_V, :] = vt[h * MLA_V:(h + 1) * MLA_V, :].astype(BF16)


def _mla_proj(xb, w1, wqn, wqp, wkn, wvt, qg, kvg, cos, sa, sb, scale, batch, seq, tm=256):
    dm = xb.shape[1]
    row = lambda b, i: (b, i, 0)
    fixed = lambda b, i: (0, 0)
    wspec = lambda w: pl.BlockSpec(w.shape, fixed, pipeline_mode=pl.Buffered(1))
    tab = lambda a: a.reshape(batch, seq, LANES)
    return pl.pallas_call(
        functools.partial(_mla_proj_kernel, scale=scale),
        grid=(batch, seq // tm),
        in_specs=[pl.BlockSpec((1, tm, dm), row), wspec(w1), wspec(wqn), wspec(wqp), wspec(wkn), wspec(wvt),
                  wspec(qg), wspec(kvg)] + [pl.BlockSpec((1, tm, LANES), row)] * 3,
        out_specs=[pl.BlockSpec((1, MLA_HEADS, tm, MLA_QK), lambda b, i: (b, 0, i, 0)),
                   pl.BlockSpec((1, MLA_HEADS, tm, MLA_QK), lambda b, i: (b, 0, i, 0)),
                   pl.BlockSpec((1, MLA_HEADS, MLA_VT_ROWS, tm), lambda b, i: (b, 0, 0, i))],
        out_shape=[jax.ShapeDtypeStruct((batch, MLA_HEADS, seq, MLA_QK), BF16),
                   jax.ShapeDtypeStruct((batch, MLA_HEADS, seq, MLA_QK), BF16),
                   jax.ShapeDtypeStruct((batch, MLA_HEADS, MLA_VT_ROWS, seq), BF16)],
        compiler_params=_params(("arbitrary",) * 2, 56),
        name="mla_projections",
    )(xb.reshape(batch, seq, dm), w1, wqn, wqp, wkn, wvt, qg, kvg, tab(cos), tab(sa), tab(sb))


def _mla_flash_kernel(qi_ref, ki_ref, q_ref, k_ref, vt_ref, z_ref, o_ref, s_buf, smax_buf, m_scr, acc_scr, *,
                      heads):
    t = pl.program_id(2)
    qi = qi_ref[t]
    ki = ki_ref[t]
    tq = q_ref.shape[2]
    tk = k_ref.shape[2]

    @pl.when(ki == 0)
    def _():
        m_scr[...] = jnp.full(m_scr.shape, -jnp.inf, F32)
        acc_scr[...] = jnp.zeros(acc_scr.shape, F32)

    qc = tq // MLA_QUERY_CHUNKS

    def scores(h, c, masked):
        cs = slice(c * qc, (c + 1) * qc)
        nk = (c + 1) * qc if masked else tk
        s = lax.dot_general(k_ref[0, h, 0:nk, :], q_ref[0, h, cs, :], (((1,), (1,)), ((), ())),
                            preferred_element_type=F32)
        if masked:
            key = lax.broadcasted_iota(jnp.int32, (nk, qc), 0)
            qry = lax.broadcasted_iota(jnp.int32, (nk, qc), 1) + c * qc
            s = jnp.where(qry >= key, s, -jnp.inf)
        s_buf[h % 2, 0:nk, cs] = s
        smax_buf[h % 2, :, cs] = jnp.max(s, axis=0, keepdims=True)

    def softmax_pv(h, c, masked):
        cs = slice(c * qc, (c + 1) * qc)
        nk = (c + 1) * qc if masked else tk
        m_prev = m_scr[h, :, cs]
        m_new = jnp.maximum(m_prev, smax_buf[h % 2, :, cs])
        alpha = jnp.exp2(m_prev - m_new)
        p = jnp.exp2(s_buf[h % 2, 0:nk, cs] - m_new)
        m_scr[h, :, cs] = m_new
        acc_scr[h, :, cs] = alpha * acc_scr[h, :, cs] + jnp.dot(
            vt_ref[0, h, :, 0:nk], p.astype(BF16), preferred_element_type=F32)

    def update(masked):
        for c in range(MLA_QUERY_CHUNKS):
            scores(0, c, masked)
        for h in range(heads):
            for c in range(MLA_QUERY_CHUNKS):
                softmax_pv(h, c, masked)
                if h + 1 < heads:
                    scores(h + 1, c, masked)

    @pl.when(ki < qi)
    def _():
        update(False)

    @pl.when(ki == qi)
    def _():
        update(True)
        for h in range(heads):
            ln = slice(h * MLA_V, (h + 1) * MLA_V)
            zz = z_ref[0, :, ln].astype(F32)
            acc = acc_scr[h]
            o = (acc[:MLA_V] * (1.0 / acc[MLA_V:MLA_V + 1])).T
            o_ref[0, :, ln] = (o * (zz / (1.0 + jnp.exp(-zz)))).astype(o_ref.dtype)


def _mla_flash(q, k, vt, z, batch, seq):
    heads = MLA_HEADS_PER_STEP
    tile = MLA_TILE
    nt = seq // tile
    qi = np.concatenate([np.full(i + 1, i, np.int32) for i in range(nt)])
    ki = np.concatenate([np.arange(i + 1, dtype=np.int32) for i in range(nt)])
    v_w = heads * MLA_V
    qmap = lambda b, h, t, qi_r, ki_r: (b, h, qi_r[t], 0)
    kmap = lambda b, h, t, qi_r, ki_r: (b, h, ki_r[t], 0)
    vmap = lambda b, h, t, qi_r, ki_r: (b, h, 0, ki_r[t])
    zmap = lambda b, h, t, qi_r, ki_r: (b, qi_r[t], h)
    grid_spec = pltpu.PrefetchScalarGridSpec(
        num_scalar_prefetch=2,
        grid=(batch, MLA_HEADS // heads, len(qi)),
        in_specs=[pl.BlockSpec((1, heads, tile, MLA_QK), qmap), pl.BlockSpec((1, heads, tile, MLA_QK), kmap),
                  pl.BlockSpec((1, heads, MLA_VT_ROWS, tile), vmap), pl.BlockSpec((1, tile, v_w), zmap)],
        out_specs=pl.BlockSpec((1, tile, v_w), zmap),
        scratch_shapes=[pltpu.VMEM((2, tile, tile), F32), pltpu.VMEM((2, 1, tile), F32),
                        pltpu.VMEM((heads, 1, tile), F32), pltpu.VMEM((heads, MLA_VT_ROWS, tile), F32)],
    )
    out = pl.pallas_call(
        functools.partial(_mla_flash_kernel, heads=heads),
        grid_spec=grid_spec,
        out_shape=jax.ShapeDtypeStruct((batch, seq, MLA_WIDTH), BF16),
        compiler_params=_params(("arbitrary",) * 3, 56),
        name="mla_flash_attention",
    )(jnp.asarray(qi), jnp.asarray(ki), q, k, vt, z.reshape(batch, seq, MLA_WIDTH))
    return out.reshape(batch * seq, MLA_WIDTH)


def _inv_freq(dim):
    return 1.0 / (ROPE_THETA ** (jnp.arange(0, dim, 2, dtype=F32) / dim))


def kernel(x, positions, dsa_w_in, dsa_w_out, mla_w_in, mla_q_norm, mla_w_uq, mla_kv_norm, mla_w_ukv,
           mla_w_out, ln_g, ln_b):
    batch, seq, dm = x.shape
    tokens = batch * seq
    half = DSA_HEAD_DIM // 2
    dils = tuple(d for _, d in DSA_PATTERNS)

    inv_a = _inv_freq(DSA_HEAD_DIM)
    inv_row_a = jnp.concatenate([inv_a, inv_a])[None, :]
    mult_a = jnp.stack([jnp.ones((LANES,), F32),
                        jnp.concatenate([-jnp.ones((half,), F32), jnp.ones((half,), F32)])])
    pos_col = positions.reshape(tokens, 1).astype(F32)
    table_nat = tuple(_rope_tables(pos_col, inv_row_a, mult_a, ("cos", "sin")))
    perm_dils = tuple(d for d in dils if d != 1)
    tables_a = dict(zip(perm_dils, _table_perm(table_nat, perm_dils, batch, seq)))
    tables_a[1] = table_nat

    inv_b = _inv_freq(MLA_ROPE)
    hb = MLA_ROPE // 2
    zeros_h = jnp.zeros((hb,), F32)
    ones_h = jnp.ones((hb,), F32)
    pad = jnp.zeros((LANES - MLA_ROPE,), F32)
    inv_row_b = jnp.concatenate([inv_b, inv_b, pad])[None, :]
    mult_b = jnp.stack([jnp.concatenate([ones_h, ones_h, pad]),
                        jnp.concatenate([zeros_h, ones_h, pad]),
                        jnp.concatenate([-ones_h, zeros_h, pad])])
    cos_b, sa_b, sb_b = _rope_tables(pos_col, inv_row_b, mult_b, ("cos", "sin", "sin"))

    x2 = x.reshape(tokens, dm)
    xb = None
    qkv_w = 3 * DSA_WIDTH
    for layer in range(DEPTH):
        j = layer // N_MIXERS
        x3 = x2.reshape(batch, seq, dm)
        if layer % N_MIXERS == 0:
            w_in = dsa_w_in[j].astype(BF16)
            if xb is None:
                xbs = dict(zip(dils, _cast_perm(x3, dils)))
            else:
                rest = tuple(d for d in dils if d != 1)
                xbs = dict(zip(rest, _cast_perm(x3, rest)))
                xbs[1] = xb
            hs = []
            q_scale = DSA_HEAD_DIM ** -0.5 * math.log2(math.e)
            for g, d in enumerate(dils):
                cos, sin = tables_a[d]
                hs.append((_mm_rope(xbs[d], w_in, g * qkv_w, cos, sin, q_scale),
                           _mm(xbs[d], w_in, g * qkv_w + 2 * DSA_WIDTH, DSA_WIDTH)))
            z = _mm(xbs[1], w_in, DSA_GROUPS * qkv_w)
            y = _dsa_attention(hs, z, batch, seq)
            w_out = dsa_w_out[j].astype(BF16)
        else:
            if xb is None:
                xb = _cast_perm(x3, (1,))[0]
            w_in = mla_w_in[j]
            o2 = MLA_Q_RANK + MLA_KV_RANK
            o3 = o2 + MLA_ROPE
            w1 = jnp.pad(w_in[:, :o3], ((0, 0), (0, LANES - MLA_ROPE))).astype(BF16)
            wz = w_in[:, o3:].astype(BF16)
            wq = mla_w_uq[j].reshape(MLA_Q_RANK, MLA_HEADS, MLA_NOPE + MLA_ROPE)
            wqn = wq[:, :, :MLA_NOPE].reshape(MLA_Q_RANK, MLA_HEADS * MLA_NOPE).astype(BF16)
            wqp = jnp.pad(wq[:, :, MLA_NOPE:], ((0, 0), (0, 0), (0, LANES - MLA_ROPE))).reshape(
                MLA_Q_RANK, MLA_HEADS * LANES).astype(BF16)
            wkv = mla_w_ukv[j].reshape(MLA_KV_RANK, MLA_HEADS, MLA_NOPE + MLA_V)
            wkn = wkv[:, :, :MLA_NOPE].reshape(MLA_KV_RANK, MLA_HEADS * MLA_NOPE).astype(BF16)
            wvt = wkv[:, :, MLA_NOPE:].reshape(MLA_KV_RANK, MLA_HEADS * MLA_V).T.astype(BF16)
            q_scale = (MLA_NOPE + MLA_ROPE) ** -0.5 * math.log2(math.e)
            q, k, vt = _mla_proj(xb, w1, wqn, wqp, wkn, wvt, mla_q_norm[j][None, :], mla_kv_norm[j][None, :],
                                 cos_b, sa_b, sb_b, q_scale, batch, seq)
            z = _mm(xb, wz)
            y = _mla_flash(q, k, vt, z, batch, seq)
            w_out = mla_w_out[j].astype(BF16)
        x2, xb = _out_ln(y, w_out, x2, ln_g[layer][None, :], ln_b[layer][None, :])
    return x2.reshape(batch, seq, dm)
```

```python
import functools
import math

import jax
import jax.numpy as jnp
import numpy as np
from jax import lax
from jax.experimental import pallas as pl
from jax.experimental.pallas import tpu as pltpu

F32 = jnp.float32
BF16 = jnp.bfloat16

D_MODEL = 2048
DEPTH = 4
N_MIXERS = 2

DSA_PATTERNS = ((128, 1), (512, 4), (2048, 16))
DSA_GROUPS = len(DSA_PATTERNS)
DSA_HEADS = 16
DSA_HEAD_DIM = 128
DSA_WIDTH = DSA_HEADS * DSA_HEAD_DIM
DSA_BLOCK = 128

MLA_HEADS = 16
MLA_Q_RANK = 512
MLA_KV_RANK = 512
MLA_NOPE = 128
MLA_ROPE = 64
MLA_V = 128
MLA_WIDTH = MLA_HEADS * MLA_V

ROPE_THETA = 10000.0
RMS_EPS = 1e-6
LN_EPS = 1e-5
DEEPNORM_ALPHA = (2 * DEPTH) ** 0.25

LANES = 128
MIB = 1024 * 1024

DSA_TILE = 2048
DSA_HEADS_PER_STEP = 2
DSA_BLOCKS_PER_ITER = 8
MLA_TILE = 1024
MLA_HEADS_PER_STEP = 8
MLA_QUERY_CHUNKS = 4
MLA_VT_ROWS = MLA_V + 16
MLA_QK = 2 * LANES


def _params(semantics, vmem_mib):
    return pltpu.CompilerParams(dimension_semantics=semantics, vmem_limit_bytes=vmem_mib * MIB)


def _rope_table_kernel(pos_ref, inv_ref, mult_ref, *out_refs, kinds):
    ang = pos_ref[...] * inv_ref[...]
    c = jnp.cos(ang)
    s = jnp.sin(ang)
    for i, (kind, o_ref) in enumerate(zip(kinds, out_refs)):
        o_ref[...] = (c if kind == "cos" else s) * mult_ref[i:i + 1, :]


def _rope_tables(pos_f32, inv_row, mult_rows, kinds, tm=2048):
    n = pos_f32.shape[0]
    nk = len(kinds)
    return pl.pallas_call(
        functools.partial(_rope_table_kernel, kinds=kinds),
        grid=(n // tm,),
        in_specs=[pl.BlockSpec((tm, 1), lambda i: (i, 0)),
                  pl.BlockSpec((1, LANES), lambda i: (0, 0)),
                  pl.BlockSpec((nk, LANES), lambda i: (0, 0))],
        out_specs=[pl.BlockSpec((tm, LANES), lambda i: (i, 0))] * nk,
        out_shape=[jax.ShapeDtypeStruct((n, LANES), F32)] * nk,
        compiler_params=_params(("arbitrary",), 32),
        name="rope_tables",
    )(pos_f32, inv_row, mult_rows)


def _table_perm_kernel(*refs, dilations, ntab):
    tabs, outs = refs[:ntab], refs[ntab:]
    tm = tabs[0].shape[1]
    for i, d in enumerate(dilations):
        for j, tab in enumerate(tabs):
            for r in range(d):
                outs[i * ntab + j][0, r] = tab[0, pl.ds(r, tm // d, stride=d), :]


def _table_perm(tabs, dilations, batch, seq, tm=2048):
    ntab = len(tabs)
    outs = pl.pallas_call(
        functools.partial(_table_perm_kernel, dilations=dilations, ntab=ntab),
        grid=(batch, seq // tm),
        in_specs=[pl.BlockSpec((1, tm, LANES), lambda b, t: (b, t, 0))] * ntab,
        out_specs=[pl.BlockSpec((1, d, tm // d, LANES), lambda b, t: (b, 0, t, 0))
                   for d in dilations for _ in range(ntab)],
        out_shape=[jax.ShapeDtypeStruct((batch, d, seq // d, LANES), F32) for d in dilations for _ in range(ntab)],
        compiler_params=_params(("arbitrary",) * 2, 32),
        name="rope_table_perm",
    )(*[t.reshape(batch, seq, LANES) for t in tabs])
    outs = [o.reshape(batch * seq, LANES) for o in outs]
    return [tuple(outs[i * ntab:(i + 1) * ntab]) for i in range(len(dilations))]


def _cast_perm_kernel(x_ref, *refs, dilations):
    o_refs, col_scr = refs[:-1], refs[-1]
    tm, dm = x_ref.shape[1], x_ref.shape[2]
    strided = any(d != 1 for d in dilations)
    for c in range(dm // LANES):
        cs = slice(c * LANES, (c + 1) * LANES)
        if strided:
            col_scr[c] = x_ref[0, :, cs]
        for d, o_ref in zip(dilations, o_refs):
            if d == 1:
                o_ref[0, 0, :, cs] = x_ref[0, :, cs].astype(BF16)
            else:
                for r in range(d):
                    o_ref[0, r, :, cs] = col_scr[c, pl.ds(r, tm // d, stride=d), :].astype(BF16)


def _cast_perm(x3, dilations, tm=1024):
    b, s, dm = x3.shape
    outs = pl.pallas_call(
        functools.partial(_cast_perm_kernel, dilations=dilations),
        grid=(b, s // tm),
        in_specs=[pl.BlockSpec((1, tm, dm), lambda bi, t: (bi, t, 0))],
        out_specs=[pl.BlockSpec((1, d, tm // d, dm), lambda bi, t: (bi, 0, t, 0)) for d in dilations],
        out_shape=[jax.ShapeDtypeStruct((b, d, s // d, dm), BF16) for d in dilations],
        scratch_shapes=[pltpu.VMEM((dm // LANES, tm, LANES), F32)],
        compiler_params=_params(("arbitrary",) * 2, 48),
        name="cast_perm",
    )(x3)
    return [o.reshape(b * s, dm) for o in outs]


def _mm_kernel(a_ref, w_ref, o_ref):
    o_ref[...] = jnp.dot(a_ref[...], w_ref[...], preferred_element_type=F32).astype(o_ref.dtype)


def _mm(a, w, col0=0, ncols=None, out_dtype=BF16, tm=2048, tn=1024):
    m, k = a.shape
    n = w.shape[1] - col0 if ncols is None else ncols
    j0 = col0 // tn
    return pl.pallas_call(
        _mm_kernel,
        grid=(m // tm, n // tn),
        in_specs=[pl.BlockSpec((tm, k), lambda i, j: (i, 0)),
                  pl.BlockSpec((k, tn), lambda i, j: (0, j + j0))],
        out_specs=pl.BlockSpec((tm, tn), lambda i, j: (i, j)),
        out_shape=jax.ShapeDtypeStruct((m, n), out_dtype),
        compiler_params=_params(("arbitrary", "arbitrary"), 56),
        name="matmul",
    )(a, w)


def _mm_rope_kernel(a_ref, w_ref, cos_ref, sin_ref, o_ref, *, n_q_blocks, q_scale):
    acc = jnp.dot(a_ref[...], w_ref[...], preferred_element_type=F32)
    scale = jnp.where(pl.program_id(1) < n_q_blocks, q_scale, 1.0).astype(F32)
    cos = cos_ref[...] * scale
    sin = sin_ref[...] * scale
    for c in range(acc.shape[1] // LANES):
        xs = acc[:, c * LANES:(c + 1) * LANES]
        o_ref[:, c * LANES:(c + 1) * LANES] = (
            xs * cos + pltpu.roll(xs, LANES // 2, 1) * sin).astype(o_ref.dtype)


def _mm_rope(a, w, col0, cos, sin, q_scale, tm=2048, tn=1024):
    m, k = a.shape
    n = 2 * DSA_WIDTH
    j0 = col0 // tn
    kern = functools.partial(_mm_rope_kernel, n_q_blocks=DSA_WIDTH // tn, q_scale=q_scale)
    return pl.pallas_call(
        kern,
        grid=(m // tm, n // tn),
        in_specs=[pl.BlockSpec((tm, k), lambda i, j: (i, 0)),
                  pl.BlockSpec((k, tn), lambda i, j: (0, j + j0)),
                  pl.BlockSpec((tm, LANES), lambda i, j: (i, 0)),
                  pl.BlockSpec((tm, LANES), lambda i, j: (i, 0))],
        out_specs=pl.BlockSpec((tm, tn), lambda i, j: (i, j)),
        out_shape=jax.ShapeDtypeStruct((m, n), BF16),
        compiler_params=_params(("arbitrary", "arbitrary"), 56),
        name="matmul_rope",
    )(a, w, cos, sin)


def _aligned(x, m):
    return x if isinstance(x, int) else pl.multiple_of(x, m)


def _band_attention(qs, kwins, vwins, biases):
    ss = [lax.dot_general(q, kw, (((1,), (1,)), ((), ())), preferred_element_type=F32) + b
          for q, kw, b in zip(qs, kwins, biases)]
    ms = [jnp.max(s, axis=1, keepdims=True) for s in ss]
    ps = [jnp.exp2(s - m) for s, m in zip(ss, ms)]
    ls = [jnp.sum(p, axis=1, keepdims=True) for p in ps]
    os_ = [jnp.dot(p.astype(BF16), vw, preferred_element_type=F32) for p, vw in zip(ps, vwins)]
    return [(o * (1.0 / l), jnp.broadcast_to(m + jnp.log2(l), o.shape)) for o, m, l in zip(os_, ms, ls)]


def _dsa_attn_kernel(*refs, dilations, spans, heads, unroll):
    ng = len(dilations)
    qkv_refs = refs[:3 * ng]
    prev_refs = refs[3 * ng:5 * ng]
    z_ref, o_ref = refs[5 * ng], refs[5 * ng + 1]
    bias_ref, o_scr, l_scr = refs[5 * ng + 2:]
    blk = DSA_BLOCK
    tile = z_ref.shape[1]
    t = pl.program_id(2)

    row = lax.broadcasted_iota(jnp.int32, (blk, 2 * blk), 0)
    col = lax.broadcasted_iota(jnp.int32, (blk, 2 * blk), 1)
    dist = row + blk - col
    for g in range(ng):
        band = (dist >= 0) & (dist <= spans[g])
        bias_ref[g, 0] = jnp.where(band, 0.0, -jnp.inf).astype(F32)
        bias_ref[g, 1] = jnp.where(band & (col >= blk), 0.0, -jnp.inf).astype(F32)
    no_prev = (t == 0).astype(jnp.int32)

    chunk = 256

    def finish(c, carry=0):
        r0 = _aligned(c * chunk, chunk)
        for hh in range(heads):
            ln = slice(hh * LANES, (hh + 1) * LANES)
            ls = [l_scr[hh, g, pl.ds(r0, chunk), :] for g in range(ng)]
            mx = functools.reduce(jnp.maximum, ls)
            es = [jnp.exp2(l - mx) for l in ls]
            den = functools.reduce(lambda a, b: a + b, es)
            num = functools.reduce(
                lambda a, b: a + b, [es[g] * o_scr[hh, g, pl.ds(r0, chunk), :] for g in range(ng)])
            zz = z_ref[0, pl.ds(r0, chunk), ln].astype(F32)
            y = (num * zz) / (den * (1.0 + jnp.exp(-zz)))
            o_ref[0, pl.ds(r0, chunk), ln] = y.astype(o_ref.dtype)
        return carry

    n_items = tile // blk
    n_iter = n_items // unroll
    order = sorted(range(ng), key=lambda g: -dilations[g])
    finished = 0
    for g in order:
        d = dilations[g]
        q_ref, k_ref, v_ref = qkv_refs[3 * g:3 * g + 3]
        kp_ref, vp_ref = prev_refs[2 * g:2 * g + 2]
        nb = n_items // d
        static_iters = nb > unroll
        assert static_iters or unroll % nb == 0

        def items(it, carry=0, g=g, d=d, q_ref=q_ref, k_ref=k_ref, v_ref=v_ref, kp_ref=kp_ref, vp_ref=vp_ref,
                  nb=nb, static_iters=static_iters):
            qs, kwins, vwins, biases, dests = [], [], [], [], []
            for u in range(unroll):
                if static_iters:
                    r, bb = divmod(it * unroll + u, nb)
                else:
                    r, bb = it * (unroll // nb) + u // nb, u % nb
                r0 = bb * blk
                rows_idx = pl.ds(r0, blk) if d == 1 else pl.ds(r0 * d + r, blk, stride=d)
                for hh in range(heads):
                    ln = slice(hh * LANES, (hh + 1) * LANES)
                    qs.append(q_ref[0, r, r0:r0 + blk, ln])
                    if bb == 0:
                        kwins.append(jnp.concatenate([kp_ref[0, r, :, ln], k_ref[0, r, 0:blk, ln]], axis=0))
                        vwins.append(jnp.concatenate([vp_ref[0, r, :, ln], v_ref[0, r, 0:blk, ln]], axis=0))
                        biases.append(bias_ref[g, no_prev])
                    else:
                        kwins.append(k_ref[0, r, r0 - blk:r0 + blk, ln])
                        vwins.append(v_ref[0, r, r0 - blk:r0 + blk, ln])
                        biases.append(bias_ref[g, 0])
                    dests.append((hh, rows_idx))
            for (o, lse), (hh, rows_idx) in zip(_band_attention(qs, kwins, vwins, biases), dests):
                o_scr[hh, g, rows_idx, :] = o
                l_scr[hh, g, rows_idx, :] = lse
            return carry

        if static_iters:
            overlap = g == order[-1] and d == 1
            per_iter = unroll * blk // chunk
            for it in range(n_iter):
                items(it)
                if overlap and it >= 1:
                    for c in range((it - 1) * per_iter, it * per_iter):
                        finish(c)
            if overlap:
                finished = (n_iter - 1) * per_iter
        else:
            lax.fori_loop(0, n_iter, items, 0)

    if finished == 0:
        lax.fori_loop(0, tile // chunk, finish, 0)
    else:
        for c in range(finished, tile // chunk):
            finish(c)


def _dsa_attention(hs, z, batch, seq):
    heads = DSA_HEADS_PER_STEP
    wl = heads * LANES
    ncol = DSA_WIDTH // wl
    tile = DSA_TILE
    blk = DSA_BLOCK
    dil = tuple(d for _, d in DSA_PATTERNS)
    spans = tuple(w // d for w, d in DSA_PATTERNS)
    for d, sp in zip(dil, spans):
        assert sp <= blk and tile % (d * blk) == 0 and seq % tile == 0
    assert (tile // blk) % DSA_BLOCKS_PER_ITER == 0

    in_specs, args, prev_specs, prev_args = [], [], [], []
    for g, d in enumerate(dil):
        hqk, hv = hs[g]
        nblk = tile // d // blk
        for kind in range(3):
            src = hv if kind == 2 else hqk
            arr = src.reshape(batch, d, seq // d, src.shape[1])
            cb = (kind % 2) * ncol
            in_specs.append(pl.BlockSpec((1, d, tile // d, wl), lambda b, h, t, cb=cb: (b, 0, t, cb + h)))
            args.append(arr)
            if kind:
                prev_specs.append(pl.BlockSpec(
                    (1, d, blk, wl), lambda b, h, t, cb=cb, nblk=nblk: (b, 0, jnp.maximum(t * nblk - 1, 0), cb + h)))
                prev_args.append(arr)
    in_specs += prev_specs
    args += prev_args
    in_specs.append(pl.BlockSpec((1, tile, wl), lambda b, h, t: (b, t, h)))
    args.append(z.reshape(batch, seq, DSA_WIDTH))
    scratch = [pltpu.VMEM((len(dil), 2, blk, 2 * blk), F32)]
    scratch += [pltpu.VMEM((heads, len(dil), tile, LANES), F32)] * 2

    out = pl.pallas_call(
        functools.partial(_dsa_attn_kernel, dilations=dil, spans=spans, heads=heads,
                          unroll=DSA_BLOCKS_PER_ITER),
        grid=(batch, ncol, seq // tile),
        in_specs=in_specs,
        out_specs=pl.BlockSpec((1, tile, wl), lambda b, h, t: (b, t, h)),
        out_shape=jax.ShapeDtypeStruct((batch, seq, DSA_WIDTH), BF16),
        scratch_shapes=scratch,
        compiler_params=_params(("arbitrary",) * 3, 56),
        name="dsa_attention",
    )(*args)
    return out.reshape(batch * seq, DSA_WIDTH)


def _out_ln_kernel(y_ref, w_ref, x_ref, g_ref, b_ref, o_ref, ob_ref):
    r = DEEPNORM_ALPHA * x_ref[...] + jnp.dot(y_ref[...], w_ref[...], preferred_element_type=F32)
    mu = jnp.mean(r, axis=-1, keepdims=True)
    xc = r - mu
    var = jnp.mean(xc * xc, axis=-1, keepdims=True)
    out = xc * lax.rsqrt(var + LN_EPS) * g_ref[...] + b_ref[...]
    o_ref[...] = out
    ob_ref[...] = out.astype(BF16)


def _out_ln(y, w, x, g, b, tm=512):
    m, k = y.shape
    n = w.shape[1]
    row = lambda i: (i, 0)
    fixed = lambda i: (0, 0)
    return pl.pallas_call(
        _out_ln_kernel,
        grid=(m // tm,),
        in_specs=[pl.BlockSpec((tm, k), row), pl.BlockSpec((k, n), fixed, pipeline_mode=pl.Buffered(1)),
                  pl.BlockSpec((tm, n), row), pl.BlockSpec((1, n), fixed), pl.BlockSpec((1, n), fixed)],
        out_specs=[pl.BlockSpec((tm, n), row), pl.BlockSpec((tm, n), row)],
        out_shape=[jax.ShapeDtypeStruct((m, n), F32), jax.ShapeDtypeStruct((m, n), BF16)],
        compiler_params=_params(("arbitrary",), 56),
        name="out_proj_layernorm",
    )(y, w, x, g, b)


def _mla_proj_kernel(x_ref, w1_ref, wqn_ref, wqp_ref, wkn_ref, wvt_ref, qg_ref, kvg_ref,
                     cos_ref, sa_ref, sb_ref, q_ref, k_ref, vt_ref, *, scale):
    h1 = jnp.dot(x_ref[0], w1_ref[...], preferred_element_type=F32)
    cq = h1[:, :MLA_Q_RANK]
    ckv = h1[:, MLA_Q_RANK:MLA_Q_RANK + MLA_KV_RANK]
    kpe = h1[:, MLA_Q_RANK + MLA_KV_RANK:]

    def rms(c, g):
        y = c * lax.rsqrt(jnp.mean(c * c, axis=-1, keepdims=True) + RMS_EPS)
        return (y * g).astype(BF16)

    cos, sa, sb = cos_ref[0], sa_ref[0], sb_ref[0]

    def rope(xs):
        return xs * cos + pltpu.roll(xs, MLA_ROPE // 2, 1) * sa + pltpu.roll(xs, LANES - MLA_ROPE // 2, 1) * sb

    cqn = rms(cq, qg_ref[...])
    ckvn = rms(ckv, kvg_ref[...])
    kpe_r = rope(kpe).astype(BF16)
    qn = jnp.dot(cqn, wqn_ref[...], preferred_element_type=F32)
    qp = jnp.dot(cqn, wqp_ref[...], preferred_element_type=F32)
    kn = jnp.dot(ckvn, wkn_ref[...], preferred_element_type=F32)
    vt = lax.dot_general(wvt_ref[...], ckvn, (((1,), (1,)), ((), ())), preferred_element_type=F32)
    extra = MLA_VT_ROWS - MLA_V
    ones_row = (lax.broadcasted_iota(jnp.int32, (extra, vt.shape[1]), 0) == 0).astype(BF16)
    for h in range(MLA_HEADS):
        vt_ref[0, h, MLA_V:MLA_VT_ROWS, :] = ones_row
        hs = slice(h * LANES, (h + 1) * LANES)
        q_ref[0, h, :, 0:LANES] = (qn[:, hs] * scale).astype(BF16)
        q_ref[0, h, :, LANES:MLA_QK] = (rope(qp[:, hs]) * scale).astype(BF16)
        k_ref[0, h, :, 0:LANES] = kn[:, hs].astype(BF16)
        k_ref[0, h, :, LANES:MLA_QK] = kpe_r
        vt_ref[0, h, 0:MLA_V, :] = vt[h * MLA_V:(h + 1) * MLA_V, :].astype(BF16)


def _mla_proj(xb, w1, wqn, wqp, wkn, wvt, qg, kvg, cos, sa, sb, scale, batch, seq, tm=256):
    dm = xb.shape[1]
    row = lambda b, i: (b, i, 0)
    fixed = lambda b, i: (0, 0)
    wspec = lambda w: pl.BlockSpec(w.shape, fixed, pipeline_mode=pl.Buffered(1))
    tab = lambda a: a.reshape(batch, seq, LANES)
    return pl.pallas_call(
        functools.partial(_mla_proj_kernel, scale=scale),
        grid=(batch, seq // tm),
        in_specs=[pl.BlockSpec((1, tm, dm), row), wspec(w1), wspec(wqn), wspec(wqp), wspec(wkn), wspec(wvt),
                  wspec(qg), wspec(kvg)] + [pl.BlockSpec((1, tm, LANES), row)] * 3,
        out_specs=[pl.BlockSpec((1, MLA_HEADS, tm, MLA_QK), lambda b, i: (b, 0, i, 0)),
                   pl.BlockSpec((1, MLA_HEADS, tm, MLA_QK), lambda b, i: (b, 0, i, 0)),
                   pl.BlockSpec((1, MLA_HEADS, MLA_VT_ROWS, tm), lambda b, i: (b, 0, 0, i))],
        out_shape=[jax.ShapeDtypeStruct((batch, MLA_HEADS, seq, MLA_QK), BF16),
                   jax.ShapeDtypeStruct((batch, MLA_HEADS, seq, MLA_QK), BF16),
                   jax.ShapeDtypeStruct((batch, MLA_HEADS, MLA_VT_ROWS, seq), BF16)],
        compiler_params=_params(("arbitrary",) * 2, 56),
        name="mla_projections",
    )(xb.reshape(batch, seq, dm), w1, wqn, wqp, wkn, wvt, qg, kvg, tab(cos), tab(sa), tab(sb))


def _mla_flash_kernel(qi_ref, ki_ref, q_ref, k_ref, vt_ref, z_ref, o_ref, s_buf, smax_buf, m_scr, acc_scr, *,
                      heads):
    t = pl.program_id(2)
    qi = qi_ref[t]
    ki = ki_ref[t]
    tq = q_ref.shape[2]
    tk = k_ref.shape[2]

    @pl.when(ki == 0)
    def _():
        m_scr[...] = jnp.full(m_scr.shape, -jnp.inf, F32)
        acc_scr[...] = jnp.zeros(acc_scr.shape, F32)

    qc = tq // MLA_QUERY_CHUNKS

    def scores(h, c, masked):
        cs = slice(c * qc, (c + 1) * qc)
        nk = (c + 1) * qc if masked else tk
        s = lax.dot_general(k_ref[0, h, 0:nk, :], q_ref[0, h, cs, :], (((1,), (1,)), ((), ())),
                            preferred_element_type=F32)
        if masked:
            key = lax.broadcasted_iota(jnp.int32, (nk, qc), 0)
            qry = lax.broadcasted_iota(jnp.int32, (nk, qc), 1) + c * qc
            s = jnp.where(qry >= key, s, -jnp.inf)
        s_buf[h % 2, 0:nk, cs] = s
        smax_buf[h % 2, :, cs] = jnp.max(s, axis=0, keepdims=True)

    def softmax(h, c, masked):
        cs = slice(c * qc, (c + 1) * qc)
        nk = (c + 1) * qc if masked else tk
        m_prev = m_scr[h, :, cs]
        m_new = jnp.maximum(m_prev, smax_buf[h % 2, :, cs])
        m_scr[h, :, cs] = m_new
        return jnp.exp2(m_prev - m_new), jnp.exp2(s_buf[h % 2, 0:nk, cs] - m_new).astype(BF16)

    def pv(h, c, masked, alpha, p):
        cs = slice(c * qc, (c + 1) * qc)
        nk = (c + 1) * qc if masked else tk
        acc_scr[h, :, cs] = alpha * acc_scr[h, :, cs] + jnp.dot(
            vt_ref[0, h, :, 0:nk], p, preferred_element_type=F32)

    def update(masked):
        for c in range(MLA_QUERY_CHUNKS):
            scores(0, c, masked)
        for h in range(heads):
            for c in range(MLA_QUERY_CHUNKS):
                alpha, p = softmax(h, c, masked)
                if h + 1 < heads:
                    scores(h + 1, c, masked)
                pv(h, c, masked, alpha, p)

    @pl.when(ki < qi)
    def _():
        update(False)

    @pl.when(ki == qi)
    def _():
        update(True)
        for h in range(heads):
            ln = slice(h * MLA_V, (h + 1) * MLA_V)
            zz = z_ref[0, :, ln].astype(F32)
            acc = acc_scr[h]
            o = (acc[:MLA_V] * (1.0 / acc[MLA_V:MLA_V + 1])).T
            o_ref[0, :, ln] = (o * (zz / (1.0 + jnp.exp(-zz)))).astype(o_ref.dtype)


def _mla_flash(q, k, vt, z, batch, seq):
    heads = MLA_HEADS_PER_STEP
    tile = MLA_TILE
    nt = seq // tile
    qi = np.concatenate([np.full(i + 1, i, np.int32) for i in range(nt)])
    ki = np.concatenate([np.arange(i + 1, dtype=np.int32) for i in range(nt)])
    v_w = heads * MLA_V
    qmap = lambda b, h, t, qi_r, ki_r: (b, h, qi_r[t], 0)
    kmap = lambda b, h, t, qi_r, ki_r: (b, h, ki_r[t], 0)
    vmap = lambda b, h, t, qi_r, ki_r: (b, h, 0, ki_r[t])
    zmap = lambda b, h, t, qi_r, ki_r: (b, qi_r[t], h)
    grid_spec = pltpu.PrefetchScalarGridSpec(
        num_scalar_prefetch=2,
        grid=(batch, MLA_HEADS // heads, len(qi)),
        in_specs=[pl.BlockSpec((1, heads, tile, MLA_QK), qmap), pl.BlockSpec((1, heads, tile, MLA_QK), kmap),
                  pl.BlockSpec((1, heads, MLA_VT_ROWS, tile), vmap), pl.BlockSpec((1, tile, v_w), zmap)],
        out_specs=pl.BlockSpec((1, tile, v_w), zmap),
        scratch_shapes=[pltpu.VMEM((2, tile, tile), F32), pltpu.VMEM((2, 1, tile), F32),
                        pltpu.VMEM((heads, 1, tile), F32), pltpu.VMEM((heads, MLA_VT_ROWS, tile), F32)],
    )
    out = pl.pallas_call(
        functools.partial(_mla_flash_kernel, heads=heads),
        grid_spec=grid_spec,
        out_shape=jax.ShapeDtypeStruct((batch, seq, MLA_WIDTH), BF16),
        compiler_params=_params(("arbitrary",) * 3, 56),
        name="mla_flash_attention",
    )(jnp.asarray(qi), jnp.asarray(ki), q, k, vt, z.reshape(batch, seq, MLA_WIDTH))
    return out.reshape(batch * seq, MLA_WIDTH)


def _inv_freq(dim):
    return 1.0 / (ROPE_THETA ** (jnp.arange(0, dim, 2, dtype=F32) / dim))


def kernel(x, positions, dsa_w_in, dsa_w_out, mla_w_in, mla_q_norm, mla_w_uq, mla_kv_norm, mla_w_ukv,
           mla_w_out, ln_g, ln_b):
    batch, seq, dm = x.shape
    tokens = batch * seq
    half = DSA_HEAD_DIM // 2
    dils = tuple(d for _, d in DSA_PATTERNS)

    inv_a = _inv_freq(DSA_HEAD_DIM)
    inv_row_a = jnp.concatenate([inv_a, inv_a])[None, :]
    mult_a = jnp.stack([jnp.ones((LANES,), F32),
                        jnp.concatenate([-jnp.ones((half,), F32), jnp.ones((half,), F32)])])
    pos_col = positions.reshape(tokens, 1).astype(F32)
    table_nat = tuple(_rope_tables(pos_col, inv_row_a, mult_a, ("cos", "sin")))
    perm_dils = tuple(d for d in dils if d != 1)
    tables_a = dict(zip(perm_dils, _table_perm(table_nat, perm_dils, batch, seq)))
    tables_a[1] = table_nat

    inv_b = _inv_freq(MLA_ROPE)
    hb = MLA_ROPE // 2
    zeros_h = jnp.zeros((hb,), F32)
    ones_h = jnp.ones((hb,), F32)
    pad = jnp.zeros((LANES - MLA_ROPE,), F32)
    inv_row_b = jnp.concatenate([inv_b, inv_b, pad])[None, :]
    mult_b = jnp.stack([jnp.concatenate([ones_h, ones_h, pad]),
                        jnp.concatenate([zeros_h, ones_h, pad]),
                        jnp.concatenate([-ones_h, zeros_h, pad])])
    cos_b, sa_b, sb_b = _rope_tables(pos_col, inv_row_b, mult_b, ("cos", "sin", "sin"))

    x2 = x.reshape(tokens, dm)
    xb = None
    qkv_w = 3 * DSA_WIDTH
    for layer in range(DEPTH):
        j = layer // N_MIXERS
        x3 = x2.reshape(batch, seq, dm)
        if layer % N_MIXERS == 0:
            w_in = dsa_w_in[j].astype(BF16)
            if xb is None:
                xbs = dict(zip(dils, _cast_perm(x3, dils)))
            else:
                rest = tuple(d for d in dils if d != 1)
                xbs = dict(zip(rest, _cast_perm(x3, rest)))
                xbs[1] = xb
            hs = []
            q_scale = DSA_HEAD_DIM ** -0.5 * math.log2(math.e)
            for g, d in enumerate(dils):
                cos, sin = tables_a[d]
                hs.append((_mm_rope(xbs[d], w_in, g * qkv_w, cos, sin, q_scale),
                           _mm(xbs[d], w_in, g * qkv_w + 2 * DSA_WIDTH, DSA_WIDTH)))
            z = _mm(xbs[1], w_in, DSA_GROUPS * qkv_w)
            y = _dsa_attention(hs, z, batch, seq)
            w_out = dsa_w_out[j].astype(BF16)
        else:
            if xb is None:
                xb = _cast_perm(x3, (1,))[0]
            w_in = mla_w_in[j]
            o2 = MLA_Q_RANK + MLA_KV_RANK
            o3 = o2 + MLA_ROPE
            w1 = jnp.pad(w_in[:, :o3], ((0, 0), (0, LANES - MLA_ROPE))).astype(BF16)
            wz = w_in[:, o3:].astype(BF16)
            wq = mla_w_uq[j].reshape(MLA_Q_RANK, MLA_HEADS, MLA_NOPE + MLA_ROPE)
            wqn = wq[:, :, :MLA_NOPE].reshape(MLA_Q_RANK, MLA_HEADS * MLA_NOPE).astype(BF16)
            wqp = jnp.pad(wq[:, :, MLA_NOPE:], ((0, 0), (0, 0), (0, LANES - MLA_ROPE))).reshape(
                MLA_Q_RANK, MLA_HEADS * LANES).astype(BF16)
            wkv = mla_w_ukv[j].reshape(MLA_KV_RANK, MLA_HEADS, MLA_NOPE + MLA_V)
            wkn = wkv[:, :, :MLA_NOPE].reshape(MLA_KV_RANK, MLA_HEADS * MLA_NOPE).astype(BF16)
            wvt = wkv[:, :, MLA_NOPE:].reshape(MLA_KV_RANK, MLA_HEADS * MLA_V).T.astype(BF16)
            q_scale = (MLA_NOPE + MLA_ROPE) ** -0.5 * math.log2(math.e)
            q, k, vt = _mla_proj(xb, w1, wqn, wqp, wkn, wvt, mla_q_norm[j][None, :], mla_kv_norm[j][None, :],
                                 cos_b, sa_b, sb_b, q_scale, batch, seq)
            z = _mm(xb, wz)
            y = _mla_flash(q, k, vt, z, batch, seq)
            w_out = mla_w_out[j].astype(BF16)
        x2, xb = _out_ln(y, w_out, x2, ln_g[layer][None, :], ln_b[layer][None, :])
    return x2.reshape(batch, seq, dm)
```

```python
import functools
import math

import jax
import jax.numpy as jnp
import numpy as np
from jax import lax
from jax.experimental import pallas as pl
from jax.experimental.pallas import tpu as pltpu

F32 = jnp.float32
BF16 = jnp.bfloat16

D_MODEL = 2048
DEPTH = 4
N_MIXERS = 2

DSA_PATTERNS = ((128, 1), (512, 4), (2048, 16))
DSA_GROUPS = len(DSA_PATTERNS)
DSA_HEADS = 16
DSA_HEAD_DIM = 128
DSA_WIDTH = DSA_HEADS * DSA_HEAD_DIM
DSA_BLOCK = 128

MLA_HEADS = 16
MLA_Q_RANK = 512
MLA_KV_RANK = 512
MLA_NOPE = 128
MLA_ROPE = 64
MLA_V = 128
MLA_WIDTH = MLA_HEADS * MLA_V

ROPE_THETA = 10000.0
RMS_EPS = 1e-6
LN_EPS = 1e-5
DEEPNORM_ALPHA = (2 * DEPTH) ** 0.25

LANES = 128
MIB = 1024 * 1024

DSA_TILE = 2048
DSA_HEADS_PER_STEP = 2
DSA_BLOCKS_PER_ITER = 8
DSA_LAST_GROUP_BLOCKS = 4
MLA_TILE = 1024
MLA_HEADS_PER_STEP = 8
MLA_QUERY_CHUNKS = 4
MLA_VT_ROWS = MLA_V + 16
MLA_QK = 2 * LANES


def _params(semantics, vmem_mib):
    return pltpu.CompilerParams(dimension_semantics=semantics, vmem_limit_bytes=vmem_mib * MIB)


def _rope_table_kernel(pos_ref, inv_ref, mult_ref, *out_refs, kinds):
    ang = pos_ref[...] * inv_ref[...]
    c = jnp.cos(ang)
    s = jnp.sin(ang)
    for i, (kind, o_ref) in enumerate(zip(kinds, out_refs)):
        o_ref[...] = (c if kind == "cos" else s) * mult_ref[i:i + 1, :]


def _rope_tables(pos_f32, inv_row, mult_rows, kinds, tm=2048):
    n = pos_f32.shape[0]
    nk = len(kinds)
    return pl.pallas_call(
        functools.partial(_rope_table_kernel, kinds=kinds),
        grid=(n // tm,),
        in_specs=[pl.BlockSpec((tm, 1), lambda i: (i, 0)),
                  pl.BlockSpec((1, LANES), lambda i: (0, 0)),
                  pl.BlockSpec((nk, LANES), lambda i: (0, 0))],
        out_specs=[pl.BlockSpec((tm, LANES), lambda i: (i, 0))] * nk,
        out_shape=[jax.ShapeDtypeStruct((n, LANES), F32)] * nk,
        compiler_params=_params(("arbitrary",), 32),
        name="rope_tables",
    )(pos_f32, inv_row, mult_rows)


def _table_perm_kernel(*refs, dilations, ntab):
    tabs, outs = refs[:ntab], refs[ntab:]
    tm = tabs[0].shape[1]
    for i, d in enumerate(dilations):
        for j, tab in enumerate(tabs):
            for r in range(d):
                outs[i * ntab + j][0, r] = tab[0, pl.ds(r, tm // d, stride=d), :]


def _table_perm(tabs, dilations, batch, seq, tm=2048):
    ntab = len(tabs)
    outs = pl.pallas_call(
        functools.partial(_table_perm_kernel, dilations=dilations, ntab=ntab),
        grid=(batch, seq // tm),
        in_specs=[pl.BlockSpec((1, tm, LANES), lambda b, t: (b, t, 0))] * ntab,
        out_specs=[pl.BlockSpec((1, d, tm // d, LANES), lambda b, t: (b, 0, t, 0))
                   for d in dilations for _ in range(ntab)],
        out_shape=[jax.ShapeDtypeStruct((batch, d, seq // d, LANES), F32) for d in dilations for _ in range(ntab)],
        compiler_params=_params(("arbitrary",) * 2, 32),
        name="rope_table_perm",
    )(*[t.reshape(batch, seq, LANES) for t in tabs])
    outs = [o.reshape(batch * seq, LANES) for o in outs]
    return [tuple(outs[i * ntab:(i + 1) * ntab]) for i in range(len(dilations))]


def _cast_perm_kernel(x_ref, *refs, dilations):
    o_refs, col_scr = refs[:-1], refs[-1]
    tm, dm = x_ref.shape[1], x_ref.shape[2]
    strided = any(d != 1 for d in dilations)
    for c in range(dm // LANES):
        cs = slice(c * LANES, (c + 1) * LANES)
        if strided:
            col_scr[c] = x_ref[0, :, cs]
        for d, o_ref in zip(dilations, o_refs):
            if d == 1:
                o_ref[0, 0, :, cs] = x_ref[0, :, cs].astype(BF16)
            else:
                for r in range(d):
                    o_ref[0, r, :, cs] = col_scr[c, pl.ds(r, tm // d, stride=d), :].astype(BF16)


def _cast_perm(x3, dilations, tm=1024):
    b, s, dm = x3.shape
    outs = pl.pallas_call(
        functools.partial(_cast_perm_kernel, dilations=dilations),
        grid=(b, s // tm),
        in_specs=[pl.BlockSpec((1, tm, dm), lambda bi, t: (bi, t, 0))],
        out_specs=[pl.BlockSpec((1, d, tm // d, dm), lambda bi, t: (bi, 0, t, 0)) for d in dilations],
        out_shape=[jax.ShapeDtypeStruct((b, d, s // d, dm), BF16) for d in dilations],
        scratch_shapes=[pltpu.VMEM((dm // LANES, tm, LANES), F32)],
        compiler_params=_params(("arbitrary",) * 2, 48),
        name="cast_perm",
    )(x3)
    return [o.reshape(b * s, dm) for o in outs]


def _mm_kernel(a_ref, w_ref, o_ref):
    o_ref[...] = jnp.dot(a_ref[...], w_ref[...], preferred_element_type=F32).astype(o_ref.dtype)


def _mm(a, w, col0=0, ncols=None, out_dtype=BF16, tm=2048, tn=1024):
    m, k = a.shape
    n = w.shape[1] - col0 if ncols is None else ncols
    j0 = col0 // tn
    return pl.pallas_call(
        _mm_kernel,
        grid=(m // tm, n // tn),
        in_specs=[pl.BlockSpec((tm, k), lambda i, j: (i, 0)),
                  pl.BlockSpec((k, tn), lambda i, j: (0, j + j0))],
        out_specs=pl.BlockSpec((tm, tn), lambda i, j: (i, j)),
        out_shape=jax.ShapeDtypeStruct((m, n), out_dtype),
        compiler_params=_params(("arbitrary", "arbitrary"), 56),
        name="matmul",
    )(a, w)


def _mm_rope_kernel(a_ref, w_ref, cos_ref, sin_ref, o_ref, *, n_q_blocks, q_scale):
    acc = jnp.dot(a_ref[...], w_ref[...], preferred_element_type=F32)
    scale = jnp.where(pl.program_id(1) < n_q_blocks, q_scale, 1.0).astype(F32)
    cos = cos_ref[...] * scale
    sin = sin_ref[...] * scale
    for c in range(acc.shape[1] // LANES):
        xs = acc[:, c * LANES:(c + 1) * LANES]
        o_ref[:, c * LANES:(c + 1) * LANES] = (
            xs * cos + pltpu.roll(xs, LANES // 2, 1) * sin).astype(o_ref.dtype)


def _mm_rope(a, w, col0, cos, sin, q_scale, tm=2048, tn=1024):
    m, k = a.shape
    n = 2 * DSA_WIDTH
    j0 = col0 // tn
    kern = functools.partial(_mm_rope_kernel, n_q_blocks=DSA_WIDTH // tn, q_scale=q_scale)
    return pl.pallas_call(
        kern,
        grid=(m // tm, n // tn),
        in_specs=[pl.BlockSpec((tm, k), lambda i, j: (i, 0)),
                  pl.BlockSpec((k, tn), lambda i, j: (0, j + j0)),
                  pl.BlockSpec((tm, LANES), lambda i, j: (i, 0)),
                  pl.BlockSpec((tm, LANES), lambda i, j: (i, 0))],
        out_specs=pl.BlockSpec((tm, tn), lambda i, j: (i, j)),
        out_shape=jax.ShapeDtypeStruct((m, n), BF16),
        compiler_params=_params(("arbitrary", "arbitrary"), 56),
        name="matmul_rope",
    )(a, w, cos, sin)


def _band_attention(qs, kwins, vwins, biases):
    ss = [lax.dot_general(q, kw, (((1,), (1,)), ((), ())), preferred_element_type=F32) + b
          for q, kw, b in zip(qs, kwins, biases)]
    ms = [jnp.max(s, axis=1, keepdims=True) for s in ss]
    ps = [jnp.exp2(s - m) for s, m in zip(ss, ms)]
    ls = [jnp.sum(p, axis=1, keepdims=True) for p in ps]
    os_ = [jnp.dot(p.astype(BF16), vw, preferred_element_type=F32) for p, vw in zip(ps, vwins)]
    return [(o * (1.0 / l), jnp.broadcast_to(m + jnp.log2(l), o.shape)) for o, m, l in zip(os_, ms, ls)]


def _dsa_attn_kernel(*refs, dilations, spans, heads, unroll):
    ng = len(dilations)
    qkv_refs = refs[:3 * ng]
    prev_refs = refs[3 * ng:5 * ng]
    z_ref, o_ref = refs[5 * ng], refs[5 * ng + 1]
    bias_ref, o_scr, l_scr = refs[5 * ng + 2:]
    blk = DSA_BLOCK
    tile = z_ref.shape[1]
    t = pl.program_id(2)

    row = lax.broadcasted_iota(jnp.int32, (blk, 2 * blk), 0)
    col = lax.broadcasted_iota(jnp.int32, (blk, 2 * blk), 1)
    dist = row + blk - col
    for g in range(ng):
        band = (dist >= 0) & (dist <= spans[g])
        bias_ref[g, 0] = jnp.where(band, 0.0, -jnp.inf).astype(F32)
        bias_ref[g, 1] = jnp.where(band & (col >= blk), 0.0, -jnp.inf).astype(F32)
    no_prev = (t == 0).astype(jnp.int32)

    chunk = 256

    def finish(c):
        r0 = c * chunk
        for hh in range(heads):
            ln = slice(hh * LANES, (hh + 1) * LANES)
            ls = [l_scr[hh, g, pl.ds(r0, chunk), :] for g in range(ng)]
            mx = functools.reduce(jnp.maximum, ls)
            es = [jnp.exp2(l - mx) for l in ls]
            den = functools.reduce(lambda a, b: a + b, es)
            num = functools.reduce(
                lambda a, b: a + b, [es[g] * o_scr[hh, g, pl.ds(r0, chunk), :] for g in range(ng)])
            zz = z_ref[0, pl.ds(r0, chunk), ln].astype(F32)
            y = (num * zz) / (den * (1.0 + jnp.exp(-zz)))
            o_ref[0, pl.ds(r0, chunk), ln] = y.astype(o_ref.dtype)

    n_items = tile // blk
    order = sorted(range(ng), key=lambda g: -dilations[g])
    last = order[-1]
    overlap = dilations[last] == 1

    def items(g, it, unroll):
        d = dilations[g]
        q_ref, k_ref, v_ref = qkv_refs[3 * g:3 * g + 3]
        kp_ref, vp_ref = prev_refs[2 * g:2 * g + 2]
        nb = n_items // d
        qs, kwins, vwins, biases, dests = [], [], [], [], []
        for u in range(unroll):
            r, bb = divmod(it * unroll + u, nb)
            r0 = bb * blk
            rows_idx = pl.ds(r0, blk) if d == 1 else pl.ds(r0 * d + r, blk, stride=d)
            for hh in range(heads):
                ln = slice(hh * LANES, (hh + 1) * LANES)
                qs.append(q_ref[0, r, r0:r0 + blk, ln])
                if bb == 0:
                    kwins.append(jnp.concatenate([kp_ref[0, r, :, ln], k_ref[0, r, 0:blk, ln]], axis=0))
                    vwins.append(jnp.concatenate([vp_ref[0, r, :, ln], v_ref[0, r, 0:blk, ln]], axis=0))
                    biases.append(bias_ref[g, no_prev])
                else:
                    kwins.append(k_ref[0, r, r0 - blk:r0 + blk, ln])
                    vwins.append(v_ref[0, r, r0 - blk:r0 + blk, ln])
                    biases.append(bias_ref[g, 0])
                dests.append((hh, rows_idx))
        for (o, lse), (hh, rows_idx) in zip(_band_attention(qs, kwins, vwins, biases), dests):
            o_scr[hh, g, rows_idx, :] = o
            l_scr[hh, g, rows_idx, :] = lse

    for it in range(n_items // unroll):
        for g in order[:-1]:
            items(g, it, unroll)
    un = DSA_LAST_GROUP_BLOCKS if overlap else unroll
    per_iter = un * blk // chunk
    n_iter = n_items // un
    for it in range(n_iter):
        items(last, it, un)
        if overlap and it >= 1:
            for c in range((it - 1) * per_iter, it * per_iter):
                finish(c)
    for c in range((n_iter - 1) * per_iter if overlap else 0, tile // chunk):
        finish(c)


def _dsa_attention(hs, z, batch, seq):
    heads = DSA_HEADS_PER_STEP
    wl = heads * LANES
    ncol = DSA_WIDTH // wl
    tile = DSA_TILE
    blk = DSA_BLOCK
    dil = tuple(d for _, d in DSA_PATTERNS)
    spans = tuple(w // d for w, d in DSA_PATTERNS)
    for d, sp in zip(dil, spans):
        assert sp <= blk and tile % (d * blk) == 0 and seq % tile == 0
    assert (tile // blk) % DSA_BLOCKS_PER_ITER == 0

    in_specs, args, prev_specs, prev_args = [], [], [], []
    for g, d in enumerate(dil):
        hqk, hv = hs[g]
        nblk = tile // d // blk
        for kind in range(3):
            src = hv if kind == 2 else hqk
            arr = src.reshape(batch, d, seq // d, src.shape[1])
            cb = (kind % 2) * ncol
            in_specs.append(pl.BlockSpec((1, d, tile // d, wl), lambda b, h, t, cb=cb: (b, 0, t, cb + h)))
            args.append(arr)
            if kind:
                prev_specs.append(pl.BlockSpec(
                    (1, d, blk, wl), lambda b, h, t, cb=cb, nblk=nblk: (b, 0, jnp.maximum(t * nblk - 1, 0), cb + h)))
                prev_args.append(arr)
    in_specs += prev_specs
    args += prev_args
    in_specs.append(pl.BlockSpec((1, tile, wl), lambda b, h, t: (b, t, h)))
    args.append(z.reshape(batch, seq, DSA_WIDTH))
    scratch = [pltpu.VMEM((len(dil), 2, blk, 2 * blk), F32)]
    scratch += [pltpu.VMEM((heads, len(dil), tile, LANES), F32)] * 2

    out = pl.pallas_call(
        functools.partial(_dsa_attn_kernel, dilations=dil, spans=spans, heads=heads,
                          unroll=DSA_BLOCKS_PER_ITER),
        grid=(batch, ncol, seq // tile),
        in_specs=in_specs,
        out_specs=pl.BlockSpec((1, tile, wl), lambda b, h, t: (b, t, h)),
        out_shape=jax.ShapeDtypeStruct((batch, seq, DSA_WIDTH), BF16),
        scratch_shapes=scratch,
        compiler_params=_params(("arbitrary",) * 3, 56),
        name="dsa_attention",
    )(*args)
    return out.reshape(batch * seq, DSA_WIDTH)


def _out_ln_kernel(y_ref, w_ref, x_ref, g_ref, b_ref, o_ref, ob_ref):
    r = DEEPNORM_ALPHA * x_ref[...] + jnp.dot(y_ref[...], w_ref[...], preferred_element_type=F32)
    mu = jnp.mean(r, axis=-1, keepdims=True)
    xc = r - mu
    var = jnp.mean(xc * xc, axis=-1, keepdims=True)
    out = xc * lax.rsqrt(var + LN_EPS) * g_ref[...] + b_ref[...]
    o_ref[...] = out
    ob_ref[...] = out.astype(BF16)


def _out_ln(y, w, x, g, b, tm=512):
    m, k = y.shape
    n = w.shape[1]
    row = lambda i: (i, 0)
    fixed = lambda i: (0, 0)
    return pl.pallas_call(
        _out_ln_kernel,
        grid=(m // tm,),
        in_specs=[pl.BlockSpec((tm, k), row), pl.BlockSpec((k, n), fixed, pipeline_mode=pl.Buffered(1)),
                  pl.BlockSpec((tm, n), row), pl.BlockSpec((1, n), fixed), pl.BlockSpec((1, n), fixed)],
        out_specs=[pl.BlockSpec((tm, n), row), pl.BlockSpec((tm, n), row)],
        out_shape=[jax.ShapeDtypeStruct((m, n), F32), jax.ShapeDtypeStruct((m, n), BF16)],
        compiler_params=_params(("arbitrary",), 56),
        name="out_proj_layernorm",
    )(y, w, x, g, b)


def _mla_proj_kernel(x_ref, w1_ref, wqn_ref, wqp_ref, wkn_ref, wvt_ref, qg_ref, kvg_ref,
                     cos_ref, sa_ref, sb_ref, q_ref, k_ref, vt_ref, *, scale):
    h1 = jnp.dot(x_ref[0], w1_ref[...], preferred_element_type=F32)
    cq = h1[:, :MLA_Q_RANK]
    ckv = h1[:, MLA_Q_RANK:MLA_Q_RANK + MLA_KV_RANK]
    kpe = h1[:, MLA_Q_RANK + MLA_KV_RANK:]

    def rms(c, g):
        y = c * lax.rsqrt(jnp.mean(c * c, axis=-1, keepdims=True) + RMS_EPS)
        return (y * g).astype(BF16)

    cos, sa, sb = cos_ref[0], sa_ref[0], sb_ref[0]

    def rope(xs):
        return xs * cos + pltpu.roll(xs, MLA_ROPE // 2, 1) * sa + pltpu.roll(xs, LANES - MLA_ROPE // 2, 1) * sb

    cqn = rms(cq, qg_ref[...])
    ckvn = rms(ckv, kvg_ref[...])
    kpe_r = rope(kpe).astype(BF16)
    qn = jnp.dot(cqn, wqn_ref[...], preferred_element_type=F32)
    qp = jnp.dot(cqn, wqp_ref[...], preferred_element_type=F32)
    kn = jnp.dot(ckvn, wkn_ref[...], preferred_element_type=F32)
    vt = lax.dot_general(wvt_ref[...], ckvn, (((1,), (1,)), ((), ())), preferred_element_type=F32)
    extra = MLA_VT_ROWS - MLA_V
    ones_row = (lax.broadcasted_iota(jnp.int32, (extra, vt.shape[1]), 0) == 0).astype(BF16)
    for h in range(MLA_HEADS):
        vt_ref[0, h, MLA_V:MLA_VT_ROWS, :] = ones_row
        hs = slice(h * LANES, (h + 1) * LANES)
        q_ref[0, h, :, 0:LANES] = (qn[:, hs] * scale).astype(BF16)
        q_ref[0, h, :, LANES:MLA_QK] = (rope(qp[:, hs]) * scale).astype(BF16)
        k_ref[0, h, :, 0:LANES] = kn[:, hs].astype(BF16)
        k_ref[0, h, :, LANES:MLA_QK] = kpe_r
        vt_ref[0, h, 0:MLA_V, :] = vt[h * MLA_V:(h + 1) * MLA_V, :].astype(BF16)


def _mla_proj(xb, w1, wqn, wqp, wkn, wvt, qg, kvg, cos, sa, sb, scale, batch, seq, tm=256):
    dm = xb.shape[1]
    row = lambda b, i: (b, i, 0)
    fixed = lambda b, i: (0, 0)
    wspec = lambda w: pl.BlockSpec(w.shape, fixed, pipeline_mode=pl.Buffered(1))
    tab = lambda a: a.reshape(batch, seq, LANES)
    return pl.pallas_call(
        functools.partial(_mla_proj_kernel, scale=scale),
        grid=(batch, seq // tm),
        in_specs=[pl.BlockSpec((1, tm, dm), row), wspec(w1), wspec(wqn), wspec(wqp), wspec(wkn), wspec(wvt),
                  wspec(qg), wspec(kvg)] + [pl.BlockSpec((1, tm, LANES), row)] * 3,
        out_specs=[pl.BlockSpec((1, MLA_HEADS, tm, MLA_QK), lambda b, i: (b, 0, i, 0)),
                   pl.BlockSpec((1, MLA_HEADS, tm, MLA_QK), lambda b, i: (b, 0, i, 0)),
                   pl.BlockSpec((1, MLA_HEADS, MLA_VT_ROWS, tm), lambda b, i: (b, 0, 0, i))],
        out_shape=[jax.ShapeDtypeStruct((batch, MLA_HEADS, seq, MLA_QK), BF16),
                   jax.ShapeDtypeStruct((batch, MLA_HEADS, seq, MLA_QK), BF16),
                   jax.ShapeDtypeStruct((batch, MLA_HEADS, MLA_VT_ROWS, seq), BF16)],
        compiler_params=_params(("arbitrary",) * 2, 56),
        name="mla_projections",
    )(xb.reshape(batch, seq, dm), w1, wqn, wqp, wkn, wvt, qg, kvg, tab(cos), tab(sa), tab(sb))


def _mla_flash_kernel(qi_ref, ki_ref, q_ref, k_ref, vt_ref, z_ref, o_ref, s_buf, smax_buf, m_scr, acc_scr, *,
                      heads):
    t = pl.program_id(2)
    qi = qi_ref[t]
    ki = ki_ref[t]
    tq = q_ref.shape[2]
    tk = k_ref.shape[2]

    @pl.when(ki == 0)
    def _():
        m_scr[...] = jnp.full(m_scr.shape, -jnp.inf, F32)
        acc_scr[...] = jnp.zeros(acc_scr.shape, F32)

    qc = tq // MLA_QUERY_CHUNKS

    def scores(h, c, masked):
        cs = slice(c * qc, (c + 1) * qc)
        nk = (c + 1) * qc if masked else tk
        s = lax.dot_general(k_ref[0, h, 0:nk, :], q_ref[0, h, cs, :], (((1,), (1,)), ((), ())),
                            preferred_element_type=F32)
        if masked:
            key = lax.broadcasted_iota(jnp.int32, (nk, qc), 0)
            qry = lax.broadcasted_iota(jnp.int32, (nk, qc), 1) + c * qc
            s = jnp.where(qry >= key, s, -jnp.inf)
        s_buf[h % 2, 0:nk, cs] = s
        smax_buf[h % 2, :, cs] = jnp.max(s, axis=0, keepdims=True)

    def softmax(h, c, masked):
        cs = slice(c * qc, (c + 1) * qc)
        nk = (c + 1) * qc if masked else tk
        m_prev = m_scr[h, :, cs]
        m_new = jnp.maximum(m_prev, smax_buf[h % 2, :, cs])
        m_scr[h, :, cs] = m_new
        return jnp.exp2(m_prev - m_new), jnp.exp2(s_buf[h % 2, 0:nk, cs] - m_new).astype(BF16)

    def pv(h, c, masked, alpha, p):
        cs = slice(c * qc, (c + 1) * qc)
        nk = (c + 1) * qc if masked else tk
        acc_scr[h, :, cs] = alpha * acc_scr[h, :, cs] + jnp.dot(
            vt_ref[0, h, :, 0:nk], p, preferred_element_type=F32)

    def update(masked):
        for c in range(MLA_QUERY_CHUNKS):
            scores(0, c, masked)
        for h in range(heads):
            for c in range(MLA_QUERY_CHUNKS):
                alpha, p = softmax(h, c, masked)
                if h + 1 < heads:
                    scores(h + 1, c, masked)
                pv(h, c, masked, alpha, p)

    @pl.when(ki < qi)
    def _():
        update(False)

    @pl.when(ki == qi)
    def _():
        update(True)
        for h in range(heads):
            ln = slice(h * MLA_V, (h + 1) * MLA_V)
            zz = z_ref[0, :, ln].astype(F32)
            acc = acc_scr[h]
            o = (acc[:MLA_V] * (1.0 / acc[MLA_V:MLA_V + 1])).T
            o_ref[0, :, ln] = (o * (zz / (1.0 + jnp.exp(-zz)))).astype(o_ref.dtype)


def _mla_flash(q, k, vt, z, batch, seq):
    heads = MLA_HEADS_PER_STEP
    tile = MLA_TILE
    nt = seq // tile
    qi = np.concatenate([np.full(i + 1, i, np.int32) for i in range(nt)])
    ki = np.concatenate([np.arange(i + 1, dtype=np.int32) for i in range(nt)])
    v_w = heads * MLA_V
    qmap = lambda b, h, t, qi_r, ki_r: (b, h, qi_r[t], 0)
    kmap = lambda b, h, t, qi_r, ki_r: (b, h, ki_r[t], 0)
    vmap = lambda b, h, t, qi_r, ki_r: (b, h, 0, ki_r[t])
    zmap = lambda b, h, t, qi_r, ki_r: (b, qi_r[t], h)
    grid_spec = pltpu.PrefetchScalarGridSpec(
        num_scalar_prefetch=2,
        grid=(batch, MLA_HEADS // heads, len(qi)),
        in_specs=[pl.BlockSpec((1, heads, tile, MLA_QK), qmap), pl.BlockSpec((1, heads, tile, MLA_QK), kmap),
                  pl.BlockSpec((1, heads, MLA_VT_ROWS, tile), vmap), pl.BlockSpec((1, tile, v_w), zmap)],
        out_specs=pl.BlockSpec((1, tile, v_w), zmap),
        scratch_shapes=[pltpu.VMEM((2, tile, tile), F32), pltpu.VMEM((2, 1, tile), F32),
                        pltpu.VMEM((heads, 1, tile), F32), pltpu.VMEM((heads, MLA_VT_ROWS, tile), F32)],
    )
    out = pl.pallas_call(
        functools.partial(_mla_flash_kernel, heads=heads),
        grid_spec=grid_spec,
        out_shape=jax.ShapeDtypeStruct((batch, seq, MLA_WIDTH), BF16),
        compiler_params=_params(("arbitrary",) * 3, 56),
        name="mla_flash_attention",
    )(jnp.asarray(qi), jnp.asarray(ki), q, k, vt, z.reshape(batch, seq, MLA_WIDTH))
    return out.reshape(batch * seq, MLA_WIDTH)


def _inv_freq(dim):
    return 1.0 / (ROPE_THETA ** (jnp.arange(0, dim, 2, dtype=F32) / dim))


def kernel(x, positions, dsa_w_in, dsa_w_out, mla_w_in, mla_q_norm, mla_w_uq, mla_kv_norm, mla_w_ukv,
           mla_w_out, ln_g, ln_b):
    batch, seq, dm = x.shape
    tokens = batch * seq
    half = DSA_HEAD_DIM // 2
    dils = tuple(d for _, d in DSA_PATTERNS)

    inv_a = _inv_freq(DSA_HEAD_DIM)
    inv_row_a = jnp.concatenate([inv_a, inv_a])[None, :]
    mult_a = jnp.stack([jnp.ones((LANES,), F32),
                        jnp.concatenate([-jnp.ones((half,), F32), jnp.ones((half,), F32)])])
    pos_col = positions.reshape(tokens, 1).astype(F32)
    table_nat = tuple(_rope_tables(pos_col, inv_row_a, mult_a, ("cos", "sin")))
    perm_dils = tuple(d for d in dils if d != 1)
    tables_a = dict(zip(perm_dils, _table_perm(table_nat, perm_dils, batch, seq)))
    tables_a[1] = table_nat

    inv_b = _inv_freq(MLA_ROPE)
    hb = MLA_ROPE // 2
    zeros_h = jnp.zeros((hb,), F32)
    ones_h = jnp.ones((hb,), F32)
    pad = jnp.zeros((LANES - MLA_ROPE,), F32)
    inv_row_b = jnp.concatenate([inv_b, inv_b, pad])[None, :]
    mult_b = jnp.stack([jnp.concatenate([ones_h, ones_h, pad]),
                        jnp.concatenate([zeros_h, ones_h, pad]),
                        jnp.concatenate([-ones_h, zeros_h, pad])])
    cos_b, sa_b, sb_b = _rope_tables(pos_col, inv_row_b, mult_b, ("cos", "sin", "sin"))

    x2 = x.reshape(tokens, dm)
    xb = None
    qkv_w = 3 * DSA_WIDTH
    for layer in range(DEPTH):
        j = layer // N_MIXERS
        x3 = x2.reshape(batch, seq, dm)
        if layer % N_MIXERS == 0:
            w_in = dsa_w_in[j].astype(BF16)
            if xb is None:
                xbs = dict(zip(dils, _cast_perm(x3, dils)))
            else:
                rest = tuple(d for d in dils if d != 1)
                xbs = dict(zip(rest, _cast_perm(x3, rest)))
                xbs[1] = xb
            hs = []
            q_scale = DSA_HEAD_DIM ** -0.5 * math.log2(math.e)
            for g, d in enumerate(dils):
                cos, sin = tables_a[d]
                hs.append((_mm_rope(xbs[d], w_in, g * qkv_w, cos, sin, q_scale),
                           _mm(xbs[d], w_in, g * qkv_w + 2 * DSA_WIDTH, DSA_WIDTH)))
            z = _mm(xbs[1], w_in, DSA_GROUPS * qkv_w)
            y = _dsa_attention(hs, z, batch, seq)
            w_out = dsa_w_out[j].astype(BF16)
        else:
            if xb is None:
                xb = _cast_perm(x3, (1,))[0]
            w_in = mla_w_in[j]
            o2 = MLA_Q_RANK + MLA_KV_RANK
            o3 = o2 + MLA_ROPE
            w1 = jnp.pad(w_in[:, :o3], ((0, 0), (0, LANES - MLA_ROPE))).astype(BF16)
            wz = w_in[:, o3:].astype(BF16)
            wq = mla_w_uq[j].reshape(MLA_Q_RANK, MLA_HEADS, MLA_NOPE + MLA_ROPE)
            wqn = wq[:, :, :MLA_NOPE].reshape(MLA_Q_RANK, MLA_HEADS * MLA_NOPE).astype(BF16)
            wqp = jnp.pad(wq[:, :, MLA_NOPE:], ((0, 0), (0, 0), (0, LANES - MLA_ROPE))).reshape(
                MLA_Q_RANK, MLA_HEADS * LANES).astype(BF16)
            wkv = mla_w_ukv[j].reshape(MLA_KV_RANK, MLA_HEADS, MLA_NOPE + MLA_V)
            wkn = wkv[:, :, :MLA_NOPE].reshape(MLA_KV_RANK, MLA_HEADS * MLA_NOPE).astype(BF16)
            wvt = wkv[:, :, MLA_NOPE:].reshape(MLA_KV_RANK, MLA_HEADS * MLA_V).T.astype(BF16)
            q_scale = (MLA_NOPE + MLA_ROPE) ** -0.5 * math.log2(math.e)
            q, k, vt = _mla_proj(xb, w1, wqn, wqp, wkn, wvt, mla_q_norm[j][None, :], mla_kv_norm[j][None, :],
                                 cos_b, sa_b, sb_b, q_scale, batch, seq)
            z = _mm(xb, wz)
            y = _mla_flash(q, k, vt, z, batch, seq)
            w_out = mla_w_out[j].astype(BF16)
        x2, xb = _out_ln(y, w_out, x2, ln_g[layer][None, :], ln_b[layer][None, :])
    return x2.reshape(batch, seq, dm)
```

```python
import functools
import math

import jax
import jax.numpy as jnp
import numpy as np
from jax import lax
from jax.experimental import pallas as pl
from jax.experimental.pallas import tpu as pltpu

F32 = jnp.float32
BF16 = jnp.bfloat16

D_MODEL = 2048
DEPTH = 4
N_MIXERS = 2

DSA_PATTERNS = ((128, 1), (512, 4), (2048, 16))
DSA_GROUPS = len(DSA_PATTERNS)
DSA_HEADS = 16
DSA_HEAD_DIM = 128
DSA_WIDTH = DSA_HEADS * DSA_HEAD_DIM
DSA_BLOCK = 128

MLA_HEADS = 16
MLA_Q_RANK = 512
MLA_KV_RANK = 512
MLA_NOPE = 128
MLA_ROPE = 64
MLA_V = 128
MLA_WIDTH = MLA_HEADS * MLA_V

ROPE_THETA = 10000.0
RMS_EPS = 1e-6
LN_EPS = 1e-5
DEEPNORM_ALPHA = (2 * DEPTH) ** 0.25

LANES = 128
MIB = 1024 * 1024

DSA_TILE = 2048
DSA_HEADS_PER_STEP = 2
DSA_BLOCKS_PER_ITER = 8
DSA_LAST_GROUP_BLOCKS = 4
MLA_TILE = 1024
MLA_HEADS_PER_STEP = 8
MLA_QUERY_CHUNKS = 4
MLA_VT_ROWS = MLA_V + 16
MLA_QK = 2 * LANES


def _params(semantics, vmem_mib):
    return pltpu.CompilerParams(dimension_semantics=semantics, vmem_limit_bytes=vmem_mib * MIB)


def _rope_table_kernel(pos_ref, inv_ref, mult_ref, *out_refs, kinds):
    ang = pos_ref[...] * inv_ref[...]
    c = jnp.cos(ang)
    s = jnp.sin(ang)
    for i, (kind, o_ref) in enumerate(zip(kinds, out_refs)):
        o_ref[...] = (c if kind == "cos" else s) * mult_ref[i:i + 1, :]


def _rope_tables(pos_f32, inv_row, mult_rows, kinds, tm=2048):
    n = pos_f32.shape[0]
    nk = len(kinds)
    return pl.pallas_call(
        functools.partial(_rope_table_kernel, kinds=kinds),
        grid=(n // tm,),
        in_specs=[pl.BlockSpec((tm, 1), lambda i: (i, 0)),
                  pl.BlockSpec((1, LANES), lambda i: (0, 0)),
                  pl.BlockSpec((nk, LANES), lambda i: (0, 0))],
        out_specs=[pl.BlockSpec((tm, LANES), lambda i: (i, 0))] * nk,
        out_shape=[jax.ShapeDtypeStruct((n, LANES), F32)] * nk,
        compiler_params=_params(("arbitrary",), 32),
        name="rope_tables",
    )(pos_f32, inv_row, mult_rows)


def _table_perm_kernel(*refs, dilations, ntab):
    tabs, outs = refs[:ntab], refs[ntab:]
    tm = tabs[0].shape[1]
    for i, d in enumerate(dilations):
        for j, tab in enumerate(tabs):
            for r in range(d):
                outs[i * ntab + j][0, r] = tab[0, pl.ds(r, tm // d, stride=d), :]


def _table_perm(tabs, dilations, batch, seq, tm=2048):
    ntab = len(tabs)
    outs = pl.pallas_call(
        functools.partial(_table_perm_kernel, dilations=dilations, ntab=ntab),
        grid=(batch, seq // tm),
        in_specs=[pl.BlockSpec((1, tm, LANES), lambda b, t: (b, t, 0))] * ntab,
        out_specs=[pl.BlockSpec((1, d, tm // d, LANES), lambda b, t: (b, 0, t, 0))
                   for d in dilations for _ in range(ntab)],
        out_shape=[jax.ShapeDtypeStruct((batch, d, seq // d, LANES), F32) for d in dilations for _ in range(ntab)],
        compiler_params=_params(("arbitrary",) * 2, 32),
        name="rope_table_perm",
    )(*[t.reshape(batch, seq, LANES) for t in tabs])
    outs = [o.reshape(batch * seq, LANES) for o in outs]
    return [tuple(outs[i * ntab:(i + 1) * ntab]) for i in range(len(dilations))]


def _cast_perm_kernel(x_ref, *refs, dilations):
    o_refs, col_scr = refs[:-1], refs[-1]
    tm, dm = x_ref.shape[1], x_ref.shape[2]
    strided = any(d != 1 for d in dilations)
    for c in range(dm // LANES):
        cs = slice(c * LANES, (c + 1) * LANES)
        if strided:
            col_scr[c] = x_ref[0, :, cs]
        for d, o_ref in zip(dilations, o_refs):
            if d == 1:
                o_ref[0, 0, :, cs] = x_ref[0, :, cs].astype(BF16)
            else:
                for r in range(d):
                    o_ref[0, r, :, cs] = col_scr[c, pl.ds(r, tm // d, stride=d), :].astype(BF16)


def _cast_perm(x3, dilations, tm=1024):
    b, s, dm = x3.shape
    outs = pl.pallas_call(
        functools.partial(_cast_perm_kernel, dilations=dilations),
        grid=(b, s // tm),
        in_specs=[pl.BlockSpec((1, tm, dm), lambda bi, t: (bi, t, 0))],
        out_specs=[pl.BlockSpec((1, d, tm // d, dm), lambda bi, t: (bi, 0, t, 0)) for d in dilations],
        out_shape=[jax.ShapeDtypeStruct((b, d, s // d, dm), BF16) for d in dilations],
        scratch_shapes=[pltpu.VMEM((dm // LANES, tm, LANES), F32)],
        compiler_params=_params(("arbitrary",) * 2, 48),
        name="cast_perm",
    )(x3)
    return [o.reshape(b * s, dm) for o in outs]


def _mm_kernel(a_ref, w_ref, o_ref):
    o_ref[...] = jnp.dot(a_ref[...], w_ref[...], preferred_element_type=F32).astype(o_ref.dtype)


def _mm(a, w, col0=0, ncols=None, out_dtype=BF16, tm=2048, tn=1024):
    m, k = a.shape
    n = w.shape[1] - col0 if ncols is None else ncols
    j0 = col0 // tn
    return pl.pallas_call(
        _mm_kernel,
        grid=(m // tm, n // tn),
        in_specs=[pl.BlockSpec((tm, k), lambda i, j: (i, 0)),
                  pl.BlockSpec((k, tn), lambda i, j: (0, j + j0))],
        out_specs=pl.BlockSpec((tm, tn), lambda i, j: (i, j)),
        out_shape=jax.ShapeDtypeStruct((m, n), out_dtype),
        compiler_params=_params(("arbitrary", "arbitrary"), 56),
        name="matmul",
    )(a, w)


def _mm_rope_kernel(a_ref, w_ref, cos_ref, sin_ref, o_ref, *, n_q_blocks, q_scale):
    acc = jnp.dot(a_ref[...], w_ref[...], preferred_element_type=F32)
    scale = jnp.where(pl.program_id(1) < n_q_blocks, q_scale, 1.0).astype(F32)
    cos = cos_ref[...] * scale
    sin = sin_ref[...] * scale
    for c in range(acc.shape[1] // LANES):
        xs = acc[:, c * LANES:(c + 1) * LANES]
        o_ref[:, c * LANES:(c + 1) * LANES] = (
            xs * cos + pltpu.roll(xs, LANES // 2, 1) * sin).astype(o_ref.dtype)


def _mm_rope(a, w, col0, cos, sin, q_scale, tm=2048, tn=1024):
    m, k = a.shape
    n = 2 * DSA_WIDTH
    j0 = col0 // tn
    kern = functools.partial(_mm_rope_kernel, n_q_blocks=DSA_WIDTH // tn, q_scale=q_scale)
    return pl.pallas_call(
        kern,
        grid=(m // tm, n // tn),
        in_specs=[pl.BlockSpec((tm, k), lambda i, j: (i, 0)),
                  pl.BlockSpec((k, tn), lambda i, j: (0, j + j0)),
                  pl.BlockSpec((tm, LANES), lambda i, j: (i, 0)),
                  pl.BlockSpec((tm, LANES), lambda i, j: (i, 0))],
        out_specs=pl.BlockSpec((tm, tn), lambda i, j: (i, j)),
        out_shape=jax.ShapeDtypeStruct((m, n), BF16),
        compiler_params=_params(("arbitrary", "arbitrary"), 56),
        name="matmul_rope",
    )(a, w, cos, sin)


def _band_attention(qs, kwins, vwins, biases):
    ss = [lax.dot_general(q, kw, (((1,), (1,)), ((), ())), preferred_element_type=F32) + b
          for q, kw, b in zip(qs, kwins, biases)]
    ms = [jnp.max(s, axis=1, keepdims=True) for s in ss]
    ps = [jnp.exp2(s - m) for s, m in zip(ss, ms)]
    ls = [jnp.sum(p, axis=1, keepdims=True) for p in ps]
    os_ = [jnp.dot(p.astype(BF16), vw, preferred_element_type=F32) for p, vw in zip(ps, vwins)]
    return [(o * (1.0 / l), jnp.broadcast_to(m + jnp.log2(l), o.shape)) for o, m, l in zip(os_, ms, ls)]


def _dsa_attn_kernel(*refs, dilations, spans, heads, unroll):
    ng = len(dilations)
    qkv_refs = refs[:3 * ng]
    prev_refs = refs[3 * ng:5 * ng]
    z_ref, o_ref = refs[5 * ng], refs[5 * ng + 1]
    bias_ref, o_scr, l_scr = refs[5 * ng + 2:]
    blk = DSA_BLOCK
    tile = z_ref.shape[1]
    t = pl.program_id(2)

    row = lax.broadcasted_iota(jnp.int32, (blk, 2 * blk), 0)
    col = lax.broadcasted_iota(jnp.int32, (blk, 2 * blk), 1)
    dist = row + blk - col
    for g in range(ng):
        band = (dist >= 0) & (dist <= spans[g])
        bias_ref[g, 0] = jnp.where(band, 0.0, -jnp.inf).astype(F32)
        bias_ref[g, 1] = jnp.where(band & (col >= blk), 0.0, -jnp.inf).astype(F32)
    no_prev = (t == 0).astype(jnp.int32)

    chunk = 256

    def finish(c):
        r0 = c * chunk
        for hh in range(heads):
            ln = slice(hh * LANES, (hh + 1) * LANES)
            ls = [l_scr[hh, g, pl.ds(r0, chunk), :] for g in range(ng)]
            mx = functools.reduce(jnp.maximum, ls)
            es = [jnp.exp2(l - mx) for l in ls]
            den = functools.reduce(lambda a, b: a + b, es)
            num = functools.reduce(
                lambda a, b: a + b, [es[g] * o_scr[hh, g, pl.ds(r0, chunk), :] for g in range(ng)])
            zz = z_ref[0, pl.ds(r0, chunk), ln].astype(F32)
            y = (num * zz) / (den * (1.0 + jnp.exp(-zz)))
            o_ref[0, pl.ds(r0, chunk), ln] = y.astype(o_ref.dtype)

    n_items = tile // blk
    order = sorted(range(ng), key=lambda g: -dilations[g])
    last = order[-1]
    overlap = dilations[last] == 1

    def items(g, it, unroll):
        d = dilations[g]
        q_ref, k_ref, v_ref = qkv_refs[3 * g:3 * g + 3]
        kp_ref, vp_ref = prev_refs[2 * g:2 * g + 2]
        nb = n_items // d
        qs, kwins, vwins, biases, dests = [], [], [], [], []
        for u in range(unroll):
            r, bb = divmod(it * unroll + u, nb)
            r0 = bb * blk
            rows_idx = pl.ds(r0, blk) if d == 1 else pl.ds(r0 * d + r, blk, stride=d)
            for hh in range(heads):
                ln = slice(hh * LANES, (hh + 1) * LANES)
                qs.append(q_ref[0, r, r0:r0 + blk, ln])
                if bb == 0:
                    kwins.append(jnp.concatenate([kp_ref[0, r, :, ln], k_ref[0, r, 0:blk, ln]], axis=0))
                    vwins.append(jnp.concatenate([vp_ref[0, r, :, ln], v_ref[0, r, 0:blk, ln]], axis=0))
                    biases.append(bias_ref[g, no_prev])
                else:
                    kwins.append(k_ref[0, r, r0 - blk:r0 + blk, ln])
                    vwins.append(v_ref[0, r, r0 - blk:r0 + blk, ln])
                    biases.append(bias_ref[g, 0])
                dests.append((hh, rows_idx))
        for (o, lse), (hh, rows_idx) in zip(_band_attention(qs, kwins, vwins, biases), dests):
            o_scr[hh, g, rows_idx, :] = o
            l_scr[hh, g, rows_idx, :] = lse

    for it in range(n_items // unroll):
        for g in order[:-1]:
            items(g, it, unroll)
    un = DSA_LAST_GROUP_BLOCKS if overlap else unroll
    per_iter = un * blk // chunk
    n_iter = n_items // un
    for it in range(n_iter):
        items(last, it, un)
        if overlap and it >= 1:
            for c in range((it - 1) * per_iter, it * per_iter):
                finish(c)
    for c in range((n_iter - 1) * per_iter if overlap else 0, tile // chunk):
        finish(c)


def _dsa_attention(hs, z, batch, seq):
    heads = DSA_HEADS_PER_STEP
    wl = heads * LANES
    ncol = DSA_WIDTH // wl
    tile = DSA_TILE
    blk = DSA_BLOCK
    dil = tuple(d for _, d in DSA_PATTERNS)
    spans = tuple(w // d for w, d in DSA_PATTERNS)
    for d, sp in zip(dil, spans):
        assert sp <= blk and tile % (d * blk) == 0 and seq % tile == 0
    assert (tile // blk) % DSA_BLOCKS_PER_ITER == 0

    in_specs, args, prev_specs, prev_args = [], [], [], []
    for g, d in enumerate(dil):
        hqk, hv = hs[g]
        nblk = tile // d // blk
        for kind in range(3):
            src = hv if kind == 2 else hqk
            arr = src.reshape(batch, d, seq // d, src.shape[1])
            cb = (kind % 2) * ncol
            in_specs.append(pl.BlockSpec((1, d, tile // d, wl), lambda b, h, t, cb=cb: (b, 0, t, cb + h)))
            args.append(arr)
            if kind:
                prev_specs.append(pl.BlockSpec(
                    (1, d, blk, wl), lambda b, h, t, cb=cb, nblk=nblk: (b, 0, jnp.maximum(t * nblk - 1, 0), cb + h)))
                prev_args.append(arr)
    in_specs += prev_specs
    args += prev_args
    in_specs.append(pl.BlockSpec((1, tile, wl), lambda b, h, t: (b, t, h)))
    args.append(z.reshape(batch, seq, DSA_WIDTH))
    scratch = [pltpu.VMEM((len(dil), 2, blk, 2 * blk), F32)]
    scratch += [pltpu.VMEM((heads, len(dil), tile, LANES), F32)] * 2

    out = pl.pallas_call(
        functools.partial(_dsa_attn_kernel, dilations=dil, spans=spans, heads=heads,
                          unroll=DSA_BLOCKS_PER_ITER),
        grid=(batch, ncol, seq // tile),
        in_specs=in_specs,
        out_specs=pl.BlockSpec((1, tile, wl), lambda b, h, t: (b, t, h)),
        out_shape=jax.ShapeDtypeStruct((batch, seq, DSA_WIDTH), BF16),
        scratch_shapes=scratch,
        compiler_params=_params(("arbitrary",) * 3, 56),
        name="dsa_attention",
    )(*args)
    return out.reshape(batch * seq, DSA_WIDTH)


def _out_ln_kernel(y_ref, w_ref, x_ref, g_ref, b_ref, o_ref, ob_ref):
    r = DEEPNORM_ALPHA * x_ref[...] + jnp.dot(y_ref[...], w_ref[...], preferred_element_type=F32)
    mu = jnp.mean(r, axis=-1, keepdims=True)
    xc = r - mu
    var = jnp.mean(xc * xc, axis=-1, keepdims=True)
    out = xc * lax.rsqrt(var + LN_EPS) * g_ref[...] + b_ref[...]
    o_ref[...] = out
    ob_ref[...] = out.astype(BF16)


def _out_ln(y, w, x, g, b, tm=512):
    m, k = y.shape
    n = w.shape[1]
    row = lambda i: (i, 0)
    fixed = lambda i: (0, 0)
    return pl.pallas_call(
        _out_ln_kernel,
        grid=(m // tm,),
        in_specs=[pl.BlockSpec((tm, k), row), pl.BlockSpec((k, n), fixed, pipeline_mode=pl.Buffered(1)),
                  pl.BlockSpec((tm, n), row), pl.BlockSpec((1, n), fixed), pl.BlockSpec((1, n), fixed)],
        out_specs=[pl.BlockSpec((tm, n), row), pl.BlockSpec((tm, n), row)],
        out_shape=[jax.ShapeDtypeStruct((m, n), F32), jax.ShapeDtypeStruct((m, n), BF16)],
        compiler_params=_params(("arbitrary",), 56),
        name="out_proj_layernorm",
    )(y, w, x, g, b)


def _mla_proj_kernel(x_ref, w1_ref, wz_ref, wqn_ref, wqp_ref, wkn_ref, wvt_ref, qg_ref, kvg_ref,
                     cos_ref, sa_ref, sb_ref, q_ref, k_ref, vt_ref, z_ref, *, scale):
    z_ref[0] = jnp.dot(x_ref[0], wz_ref[...], preferred_element_type=F32).astype(BF16)
    h1 = jnp.dot(x_ref[0], w1_ref[...], preferred_element_type=F32)
    cq = h1[:, :MLA_Q_RANK]
    ckv = h1[:, MLA_Q_RANK:MLA_Q_RANK + MLA_KV_RANK]
    kpe = h1[:, MLA_Q_RANK + MLA_KV_RANK:]

    def rms(c, g):
        y = c * lax.rsqrt(jnp.mean(c * c, axis=-1, keepdims=True) + RMS_EPS)
        return (y * g).astype(BF16)

    cos, sa, sb = cos_ref[0], sa_ref[0], sb_ref[0]

    def rope(xs):
        return xs * cos + pltpu.roll(xs, MLA_ROPE // 2, 1) * sa + pltpu.roll(xs, LANES - MLA_ROPE // 2, 1) * sb

    cqn = rms(cq, qg_ref[...])
    ckvn = rms(ckv, kvg_ref[...])
    kpe_r = rope(kpe).astype(BF16)
    qn = jnp.dot(cqn, wqn_ref[...], preferred_element_type=F32)
    qp = jnp.dot(cqn, wqp_ref[...], preferred_element_type=F32)
    kn = jnp.dot(ckvn, wkn_ref[...], preferred_element_type=F32)
    vt = lax.dot_general(wvt_ref[...], ckvn, (((1,), (1,)), ((), ())), preferred_element_type=F32)
    extra = MLA_VT_ROWS - MLA_V
    ones_row = (lax.broadcasted_iota(jnp.int32, (extra, vt.shape[1]), 0) == 0).astype(BF16)
    for h in range(MLA_HEADS):
        vt_ref[0, h, MLA_V:MLA_VT_ROWS, :] = ones_row
        hs = slice(h * LANES, (h + 1) * LANES)
        q_ref[0, h, :, 0:LANES] = (qn[:, hs] * scale).astype(BF16)
        q_ref[0, h, :, LANES:MLA_QK] = (rope(qp[:, hs]) * scale).astype(BF16)
        k_ref[0, h, :, 0:LANES] = kn[:, hs].astype(BF16)
        k_ref[0, h, :, LANES:MLA_QK] = kpe_r
        vt_ref[0, h, 0:MLA_V, :] = vt[h * MLA_V:(h + 1) * MLA_V, :].astype(BF16)


def _mla_proj(xb, w1, wz, wqn, wqp, wkn, wvt, qg, kvg, cos, sa, sb, scale, batch, seq, tm=256):
    dm = xb.shape[1]
    row = lambda b, i: (b, i, 0)
    fixed = lambda b, i: (0, 0)
    wspec = lambda w: pl.BlockSpec(w.shape, fixed, pipeline_mode=pl.Buffered(1))
    tab = lambda a: a.reshape(batch, seq, LANES)
    q, k, vt, z = pl.pallas_call(
        functools.partial(_mla_proj_kernel, scale=scale),
        grid=(batch, seq // tm),
        in_specs=[pl.BlockSpec((1, tm, dm), row), wspec(w1), wspec(wz), wspec(wqn), wspec(wqp), wspec(wkn),
                  wspec(wvt), wspec(qg), wspec(kvg)] + [pl.BlockSpec((1, tm, LANES), row)] * 3,
        out_specs=[pl.BlockSpec((1, MLA_HEADS, tm, MLA_QK), lambda b, i: (b, 0, i, 0)),
                   pl.BlockSpec((1, MLA_HEADS, tm, MLA_QK), lambda b, i: (b, 0, i, 0)),
                   pl.BlockSpec((1, MLA_HEADS, MLA_VT_ROWS, tm), lambda b, i: (b, 0, 0, i)),
                   pl.BlockSpec((1, tm, MLA_WIDTH), row)],
        out_shape=[jax.ShapeDtypeStruct((batch, MLA_HEADS, seq, MLA_QK), BF16),
                   jax.ShapeDtypeStruct((batch, MLA_HEADS, seq, MLA_QK), BF16),
                   jax.ShapeDtypeStruct((batch, MLA_HEADS, MLA_VT_ROWS, seq), BF16),
                   jax.ShapeDtypeStruct((batch, seq, MLA_WIDTH), BF16)],
        compiler_params=_params(("arbitrary",) * 2, 56),
        name="mla_projections",
    )(xb.reshape(batch, seq, dm), w1, wz, wqn, wqp, wkn, wvt, qg, kvg, tab(cos), tab(sa), tab(sb))
    return q, k, vt, z.reshape(batch * seq, MLA_WIDTH)


def _mla_flash_kernel(qi_ref, ki_ref, q_ref, k_ref, vt_ref, z_ref, o_ref, s_buf, smax_buf, m_scr, acc_scr, *,
                      heads):
    t = pl.program_id(2)
    qi = qi_ref[t]
    ki = ki_ref[t]
    tq = q_ref.shape[2]
    tk = k_ref.shape[2]

    @pl.when(ki == 0)
    def _():
        m_scr[...] = jnp.full(m_scr.shape, -jnp.inf, F32)
        acc_scr[...] = jnp.zeros(acc_scr.shape, F32)

    qc = tq // MLA_QUERY_CHUNKS

    def scores(h, c, masked):
        cs = slice(c * qc, (c + 1) * qc)
        nk = (c + 1) * qc if masked else tk
        s = lax.dot_general(k_ref[0, h, 0:nk, :], q_ref[0, h, cs, :], (((1,), (1,)), ((), ())),
                            preferred_element_type=F32)
        if masked:
            key = lax.broadcasted_iota(jnp.int32, (nk, qc), 0)
            qry = lax.broadcasted_iota(jnp.int32, (nk, qc), 1) + c * qc
            s = jnp.where(qry >= key, s, -jnp.inf)
        s_buf[h % 2, 0:nk, cs] = s
        smax_buf[h % 2, :, cs] = jnp.max(s, axis=0, keepdims=True)

    def softmax(h, c, masked):
        cs = slice(c * qc, (c + 1) * qc)
        nk = (c + 1) * qc if masked else tk
        m_prev = m_scr[h, :, cs]
        m_new = jnp.maximum(m_prev, smax_buf[h % 2, :, cs])
        m_scr[h, :, cs] = m_new
        return jnp.exp2(m_prev - m_new), jnp.exp2(s_buf[h % 2, 0:nk, cs] - m_new).astype(BF16)

    def pv(h, c, masked, alpha, p):
        cs = slice(c * qc, (c + 1) * qc)
        nk = (c + 1) * qc if masked else tk
        acc_scr[h, :, cs] = alpha * acc_scr[h, :, cs] + jnp.dot(
            vt_ref[0, h, :, 0:nk], p, preferred_element_type=F32)

    def update(masked):
        for c in range(MLA_QUERY_CHUNKS):
            scores(0, c, masked)
        for h in range(heads):
            for c in range(MLA_QUERY_CHUNKS):
                alpha, p = softmax(h, c, masked)
                if h + 1 < heads:
                    scores(h + 1, c, masked)
                pv(h, c, masked, alpha, p)

    @pl.when(ki < qi)
    def _():
        update(False)

    @pl.when(ki == qi)
    def _():
        update(True)
        for h in range(heads):
            ln = slice(h * MLA_V, (h + 1) * MLA_V)
            zz = z_ref[0, :, ln].astype(F32)
            acc = acc_scr[h]
            o = (acc[:MLA_V] * (1.0 / acc[MLA_V:MLA_V + 1])).T
            o_ref[0, :, ln] = (o * (zz / (1.0 + jnp.exp(-zz)))).astype(o_ref.dtype)


def _mla_flash(q, k, vt, z, batch, seq):
    heads = MLA_HEADS_PER_STEP
    tile = MLA_TILE
    nt = seq // tile
    qi = np.concatenate([np.full(i + 1, i, np.int32) for i in range(nt)])
    ki = np.concatenate([np.arange(i + 1, dtype=np.int32) for i in range(nt)])
    v_w = heads * MLA_V
    qmap = lambda b, h, t, qi_r, ki_r: (b, h, qi_r[t], 0)
    kmap = lambda b, h, t, qi_r, ki_r: (b, h, ki_r[t], 0)
    vmap = lambda b, h, t, qi_r, ki_r: (b, h, 0, ki_r[t])
    zmap = lambda b, h, t, qi_r, ki_r: (b, qi_r[t], h)
    grid_spec = pltpu.PrefetchScalarGridSpec(
        num_scalar_prefetch=2,
        grid=(batch, MLA_HEADS // heads, len(qi)),
        in_specs=[pl.BlockSpec((1, heads, tile, MLA_QK), qmap), pl.BlockSpec((1, heads, tile, MLA_QK), kmap),
                  pl.BlockSpec((1, heads, MLA_VT_ROWS, tile), vmap), pl.BlockSpec((1, tile, v_w), zmap)],
        out_specs=pl.BlockSpec((1, tile, v_w), zmap),
        scratch_shapes=[pltpu.VMEM((2, tile, tile), F32), pltpu.VMEM((2, 1, tile), F32),
                        pltpu.VMEM((heads, 1, tile), F32), pltpu.VMEM((heads, MLA_VT_ROWS, tile), F32)],
    )
    out = pl.pallas_call(
        functools.partial(_mla_flash_kernel, heads=heads),
        grid_spec=grid_spec,
        out_shape=jax.ShapeDtypeStruct((batch, seq, MLA_WIDTH), BF16),
        compiler_params=_params(("arbitrary",) * 3, 56),
        name="mla_flash_attention",
    )(jnp.asarray(qi), jnp.asarray(ki), q, k, vt, z.reshape(batch, seq, MLA_WIDTH))
    return out.reshape(batch * seq, MLA_WIDTH)


def _inv_freq(dim):
    return 1.0 / (ROPE_THETA ** (jnp.arange(0, dim, 2, dtype=F32) / dim))


def kernel(x, positions, dsa_w_in, dsa_w_out, mla_w_in, mla_q_norm, mla_w_uq, mla_kv_norm, mla_w_ukv,
           mla_w_out, ln_g, ln_b):
    batch, seq, dm = x.shape
    tokens = batch * seq
    half = DSA_HEAD_DIM // 2
    dils = tuple(d for _, d in DSA_PATTERNS)

    inv_a = _inv_freq(DSA_HEAD_DIM)
    inv_row_a = jnp.concatenate([inv_a, inv_a])[None, :]
    mult_a = jnp.stack([jnp.ones((LANES,), F32),
                        jnp.concatenate([-jnp.ones((half,), F32), jnp.ones((half,), F32)])])
    pos_col = positions.reshape(tokens, 1).astype(F32)
    table_nat = tuple(_rope_tables(pos_col, inv_row_a, mult_a, ("cos", "sin")))
    perm_dils = tuple(d for d in dils if d != 1)
    tables_a = dict(zip(perm_dils, _table_perm(table_nat, perm_dils, batch, seq)))
    tables_a[1] = table_nat

    inv_b = _inv_freq(MLA_ROPE)
    hb = MLA_ROPE // 2
    zeros_h = jnp.zeros((hb,), F32)
    ones_h = jnp.ones((hb,), F32)
    pad = jnp.zeros((LANES - MLA_ROPE,), F32)
    inv_row_b = jnp.concatenate([inv_b, inv_b, pad])[None, :]
    mult_b = jnp.stack([jnp.concatenate([ones_h, ones_h, pad]),
                        jnp.concatenate([zeros_h, ones_h, pad]),
                        jnp.concatenate([-ones_h, zeros_h, pad])])
    cos_b, sa_b, sb_b = _rope_tables(pos_col, inv_row_b, mult_b, ("cos", "sin", "sin"))

    x2 = x.reshape(tokens, dm)
    xb = None
    qkv_w = 3 * DSA_WIDTH
    for layer in range(DEPTH):
        j = layer // N_MIXERS
        x3 = x2.reshape(batch, seq, dm)
        if layer % N_MIXERS == 0:
            w_in = dsa_w_in[j].astype(BF16)
            if xb is None:
                xbs = dict(zip(dils, _cast_perm(x3, dils)))
            else:
                rest = tuple(d for d in dils if d != 1)
                xbs = dict(zip(rest, _cast_perm(x3, rest)))
                xbs[1] = xb
            hs = []
            q_scale = DSA_HEAD_DIM ** -0.5 * math.log2(math.e)
            for g, d in enumerate(dils):
                cos, sin = tables_a[d]
                hs.append((_mm_rope(xbs[d], w_in, g * qkv_w, cos, sin, q_scale),
                           _mm(xbs[d], w_in, g * qkv_w + 2 * DSA_WIDTH, DSA_WIDTH)))
            z = _mm(xbs[1], w_in, DSA_GROUPS * qkv_w)
            y = _dsa_attention(hs, z, batch, seq)
            w_out = dsa_w_out[j].astype(BF16)
        else:
            if xb is None:
                xb = _cast_perm(x3, (1,))[0]
            w_in = mla_w_in[j]
            o2 = MLA_Q_RANK + MLA_KV_RANK
            o3 = o2 + MLA_ROPE
            w1 = jnp.pad(w_in[:, :o3], ((0, 0), (0, LANES - MLA_ROPE))).astype(BF16)
            wz = w_in[:, o3:].astype(BF16)
            wq = mla_w_uq[j].reshape(MLA_Q_RANK, MLA_HEADS, MLA_NOPE + MLA_ROPE)
            wqn = wq[:, :, :MLA_NOPE].reshape(MLA_Q_RANK, MLA_HEADS * MLA_NOPE).astype(BF16)
            wqp = jnp.pad(wq[:, :, MLA_NOPE:], ((0, 0), (0, 0), (0, LANES - MLA_ROPE))).reshape(
                MLA_Q_RANK, MLA_HEADS * LANES).astype(BF16)
            wkv = mla_w_ukv[j].reshape(MLA_KV_RANK, MLA_HEADS, MLA_NOPE + MLA_V)
            wkn = wkv[:, :, :MLA_NOPE].reshape(MLA_KV_RANK, MLA_HEADS * MLA_NOPE).astype(BF16)
            wvt = wkv[:, :, MLA_NOPE:].reshape(MLA_KV_RANK, MLA_HEADS * MLA_V).T.astype(BF16)
            q_scale = (MLA_NOPE + MLA_ROPE) ** -0.5 * math.log2(math.e)
            q, k, vt, z = _mla_proj(xb, w1, wz, wqn, wqp, wkn, wvt, mla_q_norm[j][None, :],
                                    mla_kv_norm[j][None, :], cos_b, sa_b, sb_b, q_scale, batch, seq)
            y = _mla_flash(q, k, vt, z, batch, seq)
            w_out = mla_w_out[j].astype(BF16)
        x2, xb = _out_ln(y, w_out, x2, ln_g[layer][None, :], ln_b[layer][None, :])
    return x2.reshape(batch, seq, dm)
```

```python
import functools
import math

import jax
import jax.numpy as jnp
import numpy as np
from jax import lax
from jax.experimental import pallas as pl
from jax.experimental.pallas import tpu as pltpu

F32 = jnp.float32
BF16 = jnp.bfloat16

D_MODEL = 2048
DEPTH = 4
N_MIXERS = 2

DSA_PATTERNS = ((128, 1), (512, 4), (2048, 16))
DSA_GROUPS = len(DSA_PATTERNS)
DSA_HEADS = 16
DSA_HEAD_DIM = 128
DSA_WIDTH = DSA_HEADS * DSA_HEAD_DIM
DSA_BLOCK = 128

MLA_HEADS = 16
MLA_Q_RANK = 512
MLA_KV_RANK = 512
MLA_NOPE = 128
MLA_ROPE = 64
MLA_V = 128
MLA_WIDTH = MLA_HEADS * MLA_V

ROPE_THETA = 10000.0
RMS_EPS = 1e-6
LN_EPS = 1e-5
DEEPNORM_ALPHA = (2 * DEPTH) ** 0.25

LANES = 128
MIB = 1024 * 1024

DSA_TILE = 2048
DSA_HEADS_PER_STEP = 2
DSA_BLOCKS_PER_ITER = 8
DSA_LAST_GROUP_BLOCKS = 4
MLA_TILE = 1024
MLA_HEADS_PER_STEP = 8
MLA_QUERY_CHUNKS = 4
MLA_VT_ROWS = MLA_V + 16
MLA_QK = 2 * LANES


def _params(semantics, vmem_mib, fuse_inputs=None):
    return pltpu.CompilerParams(dimension_semantics=semantics, vmem_limit_bytes=vmem_mib * MIB,
                                allow_input_fusion=fuse_inputs)


def _rope_table_kernel(pos_ref, inv_ref, mult_ref, *out_refs, kinds):
    ang = pos_ref[...] * inv_ref[...]
    c = jnp.cos(ang)
    s = jnp.sin(ang)
    for i, (kind, o_ref) in enumerate(zip(kinds, out_refs)):
        o_ref[...] = (c if kind == "cos" else s) * mult_ref[i:i + 1, :]


def _rope_tables(pos_f32, inv_row, mult_rows, kinds, tm=2048):
    n = pos_f32.shape[0]
    nk = len(kinds)
    return pl.pallas_call(
        functools.partial(_rope_table_kernel, kinds=kinds),
        grid=(n // tm,),
        in_specs=[pl.BlockSpec((tm, 1), lambda i: (i, 0)),
                  pl.BlockSpec((1, LANES), lambda i: (0, 0)),
                  pl.BlockSpec((nk, LANES), lambda i: (0, 0))],
        out_specs=[pl.BlockSpec((tm, LANES), lambda i: (i, 0))] * nk,
        out_shape=[jax.ShapeDtypeStruct((n, LANES), F32)] * nk,
        compiler_params=_params(("arbitrary",), 32),
        name="rope_tables",
    )(pos_f32, inv_row, mult_rows)


def _table_perm_kernel(*refs, dilations, ntab):
    tabs, outs = refs[:ntab], refs[ntab:]
    tm = tabs[0].shape[1]
    for i, d in enumerate(dilations):
        for j, tab in enumerate(tabs):
            for r in range(d):
                outs[i * ntab + j][0, r] = tab[0, pl.ds(r, tm // d, stride=d), :]


def _table_perm(tabs, dilations, batch, seq, tm=2048):
    ntab = len(tabs)
    outs = pl.pallas_call(
        functools.partial(_table_perm_kernel, dilations=dilations, ntab=ntab),
        grid=(batch, seq // tm),
        in_specs=[pl.BlockSpec((1, tm, LANES), lambda b, t: (b, t, 0))] * ntab,
        out_specs=[pl.BlockSpec((1, d, tm // d, LANES), lambda b, t: (b, 0, t, 0))
                   for d in dilations for _ in range(ntab)],
        out_shape=[jax.ShapeDtypeStruct((batch, d, seq // d, LANES), F32) for d in dilations for _ in range(ntab)],
        compiler_params=_params(("arbitrary",) * 2, 32),
        name="rope_table_perm",
    )(*[t.reshape(batch, seq, LANES) for t in tabs])
    outs = [o.reshape(batch * seq, LANES) for o in outs]
    return [tuple(outs[i * ntab:(i + 1) * ntab]) for i in range(len(dilations))]


def _cast_perm_kernel(x_ref, *refs, dilations):
    o_refs, col_scr = refs[:-1], refs[-1]
    tm, dm = x_ref.shape[1], x_ref.shape[2]
    strided = any(d != 1 for d in dilations)
    for c in range(dm // LANES):
        cs = slice(c * LANES, (c + 1) * LANES)
        if strided:
            col_scr[c] = x_ref[0, :, cs]
        for d, o_ref in zip(dilations, o_refs):
            if d == 1:
                o_ref[0, 0, :, cs] = x_ref[0, :, cs].astype(BF16)
            else:
                for r in range(d):
                    o_ref[0, r, :, cs] = col_scr[c, pl.ds(r, tm // d, stride=d), :].astype(BF16)


def _cast_perm(x3, dilations, tm=1024):
    b, s, dm = x3.shape
    outs = pl.pallas_call(
        functools.partial(_cast_perm_kernel, dilations=dilations),
        grid=(b, s // tm),
        in_specs=[pl.BlockSpec((1, tm, dm), lambda bi, t: (bi, t, 0))],
        out_specs=[pl.BlockSpec((1, d, tm // d, dm), lambda bi, t: (bi, 0, t, 0)) for d in dilations],
        out_shape=[jax.ShapeDtypeStruct((b, d, s // d, dm), BF16) for d in dilations],
        scratch_shapes=[pltpu.VMEM((dm // LANES, tm, LANES), F32)],
        compiler_params=_params(("arbitrary",) * 2, 48),
        name="cast_perm",
    )(x3)
    return [o.reshape(b * s, dm) for o in outs]


def _mm_kernel(a_ref, w_ref, o_ref):
    o_ref[...] = jnp.dot(a_ref[...], w_ref[...], preferred_element_type=F32).astype(o_ref.dtype)


def _mm(a, w, col0=0, ncols=None, out_dtype=BF16, tm=2048, tn=1024):
    m, k = a.shape
    n = w.shape[1] - col0 if ncols is None else ncols
    j0 = col0 // tn
    return pl.pallas_call(
        _mm_kernel,
        grid=(m // tm, n // tn),
        in_specs=[pl.BlockSpec((tm, k), lambda i, j: (i, 0)),
                  pl.BlockSpec((k, tn), lambda i, j: (0, j + j0))],
        out_specs=pl.BlockSpec((tm, tn), lambda i, j: (i, j)),
        out_shape=jax.ShapeDtypeStruct((m, n), out_dtype),
        compiler_params=_params(("arbitrary", "arbitrary"), 56, fuse_inputs=[False, True]),
        name="matmul",
    )(a, w)


def _mm_rope_kernel(a_ref, w_ref, cos_ref, sin_ref, o_ref, *, n_q_blocks, q_scale):
    acc = jnp.dot(a_ref[...], w_ref[...], preferred_element_type=F32)
    scale = jnp.where(pl.program_id(1) < n_q_blocks, q_scale, 1.0).astype(F32)
    cos = cos_ref[...] * scale
    sin = sin_ref[...] * scale
    for c in range(acc.shape[1] // LANES):
        xs = acc[:, c * LANES:(c + 1) * LANES]
        o_ref[:, c * LANES:(c + 1) * LANES] = (
            xs * cos + pltpu.roll(xs, LANES // 2, 1) * sin).astype(o_ref.dtype)


def _mm_rope(a, w, col0, cos, sin, q_scale, tm=2048, tn=1024):
    m, k = a.shape
    n = 2 * DSA_WIDTH
    j0 = col0 // tn
    kern = functools.partial(_mm_rope_kernel, n_q_blocks=DSA_WIDTH // tn, q_scale=q_scale)
    return pl.pallas_call(
        kern,
        grid=(m // tm, n // tn),
        in_specs=[pl.BlockSpec((tm, k), lambda i, j: (i, 0)),
                  pl.BlockSpec((k, tn), lambda i, j: (0, j + j0)),
                  pl.BlockSpec((tm, LANES), lambda i, j: (i, 0)),
                  pl.BlockSpec((tm, LANES), lambda i, j: (i, 0))],
        out_specs=pl.BlockSpec((tm, tn), lambda i, j: (i, j)),
        out_shape=jax.ShapeDtypeStruct((m, n), BF16),
        compiler_params=_params(("arbitrary", "arbitrary"), 56, fuse_inputs=[False, True, False, False]),
        name="matmul_rope",
    )(a, w, cos, sin)


def _band_attention(qs, kwins, vwins, biases):
    ss = [lax.dot_general(q, kw, (((1,), (1,)), ((), ())), preferred_element_type=F32) + b
          for q, kw, b in zip(qs, kwins, biases)]
    ms = [jnp.max(s, axis=1, keepdims=True) for s in ss]
    ps = [jnp.exp2(s - m) for s, m in zip(ss, ms)]
    ls = [jnp.sum(p, axis=1, keepdims=True) for p in ps]
    os_ = [jnp.dot(p.astype(BF16), vw, preferred_element_type=F32) for p, vw in zip(ps, vwins)]
    return [(o * (1.0 / l), jnp.broadcast_to(m + jnp.log2(l), o.shape)) for o, m, l in zip(os_, ms, ls)]


def _dsa_attn_kernel(*refs, dilations, spans, heads, unroll):
    ng = len(dilations)
    qkv_refs = refs[:3 * ng]
    prev_refs = refs[3 * ng:5 * ng]
    z_ref, o_ref = refs[5 * ng], refs[5 * ng + 1]
    bias_ref, o_scr, l_scr = refs[5 * ng + 2:]
    blk = DSA_BLOCK
    tile = z_ref.shape[1]
    t = pl.program_id(2)

    row = lax.broadcasted_iota(jnp.int32, (blk, 2 * blk), 0)
    col = lax.broadcasted_iota(jnp.int32, (blk, 2 * blk), 1)
    dist = row + blk - col
    for g in range(ng):
        band = (dist >= 0) & (dist <= spans[g])
        bias_ref[g, 0] = jnp.where(band, 0.0, -jnp.inf).astype(F32)
        bias_ref[g, 1] = jnp.where(band & (col >= blk), 0.0, -jnp.inf).astype(F32)
    no_prev = (t == 0).astype(jnp.int32)

    chunk = 256

    def finish(c):
        r0 = c * chunk
        for hh in range(heads):
            ln = slice(hh * LANES, (hh + 1) * LANES)
            ls = [l_scr[hh, g, pl.ds(r0, chunk), :] for g in range(ng)]
            mx = functools.reduce(jnp.maximum, ls)
            es = [jnp.exp2(l - mx) for l in ls]
            den = functools.reduce(lambda a, b: a + b, es)
            num = functools.reduce(
                lambda a, b: a + b, [es[g] * o_scr[hh, g, pl.ds(r0, chunk), :] for g in range(ng)])
            zz = z_ref[0, pl.ds(r0, chunk), ln].astype(F32)
            y = (num * zz) / (den * (1.0 + jnp.exp(-zz)))
            o_ref[0, pl.ds(r0, chunk), ln] = y.astype(o_ref.dtype)

    n_items = tile // blk
    order = sorted(range(ng), key=lambda g: -dilations[g])
    last = order[-1]
    overlap = dilations[last] == 1

    def items(g, it, unroll):
        d = dilations[g]
        q_ref, k_ref, v_ref = qkv_refs[3 * g:3 * g + 3]
        kp_ref, vp_ref = prev_refs[2 * g:2 * g + 2]
        nb = n_items // d
        qs, kwins, vwins, biases, dests = [], [], [], [], []
        for u in range(unroll):
            r, bb = divmod(it * unroll + u, nb)
            r0 = bb * blk
            rows_idx = pl.ds(r0, blk) if d == 1 else pl.ds(r0 * d + r, blk, stride=d)
            for hh in range(heads):
                ln = slice(hh * LANES, (hh + 1) * LANES)
                qs.append(q_ref[0, r, r0:r0 + blk, ln])
                if bb == 0:
                    kwins.append(jnp.concatenate([kp_ref[0, r, :, ln], k_ref[0, r, 0:blk, ln]], axis=0))
                    vwins.append(jnp.concatenate([vp_ref[0, r, :, ln], v_ref[0, r, 0:blk, ln]], axis=0))
                    biases.append(bias_ref[g, no_prev])
                else:
                    kwins.append(k_ref[0, r, r0 - blk:r0 + blk, ln])
                    vwins.append(v_ref[0, r, r0 - blk:r0 + blk, ln])
                    biases.append(bias_ref[g, 0])
                dests.append((hh, rows_idx))
        for (o, lse), (hh, rows_idx) in zip(_band_attention(qs, kwins, vwins, biases), dests):
            o_scr[hh, g, rows_idx, :] = o
            l_scr[hh, g, rows_idx, :] = lse

    for it in range(n_items // unroll):
        for g in order[:-1]:
            items(g, it, unroll)
    un = DSA_LAST_GROUP_BLOCKS if overlap else unroll
    per_iter = un * blk // chunk
    n_iter = n_items // un
    for it in range(n_iter):
        items(last, it, un)
        if overlap and it >= 1:
            for c in range((it - 1) * per_iter, it * per_iter):
                finish(c)
    for c in range((n_iter - 1) * per_iter if overlap else 0, tile // chunk):
        finish(c)


def _dsa_attention(hs, z, batch, seq):
    heads = DSA_HEADS_PER_STEP
    wl = heads * LANES
    ncol = DSA_WIDTH // wl
    tile = DSA_TILE
    blk = DSA_BLOCK
    dil = tuple(d for _, d in DSA_PATTERNS)
    spans = tuple(w // d for w, d in DSA_PATTERNS)
    for d, sp in zip(dil, spans):
        assert sp <= blk and tile % (d * blk) == 0 and seq % tile == 0
    assert (tile // blk) % DSA_BLOCKS_PER_ITER == 0

    in_specs, args, prev_specs, prev_args = [], [], [], []
    for g, d in enumerate(dil):
        hqk, hv = hs[g]
        nblk = tile // d // blk
        for kind in range(3):
            src = hv if kind == 2 else hqk
            arr = src.reshape(batch, d, seq // d, src.shape[1])
            cb = (kind % 2) * ncol
            in_specs.append(pl.BlockSpec((1, d, tile // d, wl), lambda b, h, t, cb=cb: (b, 0, t, cb + h)))
            args.append(arr)
            if kind:
                prev_specs.append(pl.BlockSpec(
                    (1, d, blk, wl), lambda b, h, t, cb=cb, nblk=nblk: (b, 0, jnp.maximum(t * nblk - 1, 0), cb + h)))
                prev_args.append(arr)
    in_specs += prev_specs
    args += prev_args
    in_specs.append(pl.BlockSpec((1, tile, wl), lambda b, h, t: (b, t, h)))
    args.append(z.reshape(batch, seq, DSA_WIDTH))
    scratch = [pltpu.VMEM((len(dil), 2, blk, 2 * blk), F32)]
    scratch += [pltpu.VMEM((heads, len(dil), tile, LANES), F32)] * 2

    out = pl.pallas_call(
        functools.partial(_dsa_attn_kernel, dilations=dil, spans=spans, heads=heads,
                          unroll=DSA_BLOCKS_PER_ITER),
        grid=(batch, ncol, seq // tile),
        in_specs=in_specs,
        out_specs=pl.BlockSpec((1, tile, wl), lambda b, h, t: (b, t, h)),
        out_shape=jax.ShapeDtypeStruct((batch, seq, DSA_WIDTH), BF16),
        scratch_shapes=scratch,
        compiler_params=_params(("arbitrary",) * 3, 56),
        name="dsa_attention",
    )(*args)
    return out.reshape(batch * seq, DSA_WIDTH)


def _out_ln_kernel(y_ref, w_ref, x_ref, g_ref, b_ref, o_ref, ob_ref):
    r = DEEPNORM_ALPHA * x_ref[...] + jnp.dot(y_ref[...], w_ref[...], preferred_element_type=F32)
    mu = jnp.mean(r, axis=-1, keepdims=True)
    xc = r - mu
    var = jnp.mean(xc * xc, axis=-1, keepdims=True)
    out = xc * lax.rsqrt(var + LN_EPS) * g_ref[...] + b_ref[...]
    o_ref[...] = out
    ob_ref[...] = out.astype(BF16)


def _out_ln(y, w, x, g, b, tm=512):
    m, k = y.shape
    n = w.shape[1]
    row = lambda i: (i, 0)
    fixed = lambda i: (0, 0)
    return pl.pallas_call(
        _out_ln_kernel,
        grid=(m // tm,),
        in_specs=[pl.BlockSpec((tm, k), row), pl.BlockSpec((k, n), fixed, pipeline_mode=pl.Buffered(1)),
                  pl.BlockSpec((tm, n), row), pl.BlockSpec((1, n), fixed), pl.BlockSpec((1, n), fixed)],
        out_specs=[pl.BlockSpec((tm, n), row), pl.BlockSpec((tm, n), row)],
        out_shape=[jax.ShapeDtypeStruct((m, n), F32), jax.ShapeDtypeStruct((m, n), BF16)],
        compiler_params=_params(("arbitrary",), 56),
        name="out_proj_layernorm",
    )(y, w, x, g, b)


def _mla_proj_kernel(x_ref, w1_ref, wz_ref, wqn_ref, wqp_ref, wkn_ref, wvt_ref, qg_ref, kvg_ref,
                     cos_ref, sa_ref, sb_ref, q_ref, k_ref, vt_ref, z_ref, *, scale):
    z_ref[0] = jnp.dot(x_ref[0], wz_ref[...], preferred_element_type=F32).astype(BF16)
    h1 = jnp.dot(x_ref[0], w1_ref[...], preferred_element_type=F32)
    cq = h1[:, :MLA_Q_RANK]
    ckv = h1[:, MLA_Q_RANK:MLA_Q_RANK + MLA_KV_RANK]
    kpe = h1[:, MLA_Q_RANK + MLA_KV_RANK:]

    def rms(c, g):
        y = c * lax.rsqrt(jnp.mean(c * c, axis=-1, keepdims=True) + RMS_EPS)
        return (y * g).astype(BF16)

    cos, sa, sb = cos_ref[0], sa_ref[0], sb_ref[0]

    def rope(xs):
        return xs * cos + pltpu.roll(xs, MLA_ROPE // 2, 1) * sa + pltpu.roll(xs, LANES - MLA_ROPE // 2, 1) * sb

    cqn = rms(cq, qg_ref[...])
    ckvn = rms(ckv, kvg_ref[...])
    kpe_r = rope(kpe).astype(BF16)
    qn = jnp.dot(cqn, wqn_ref[...], preferred_element_type=F32)
    qp = jnp.dot(cqn, wqp_ref[...], preferred_element_type=F32)
    kn = jnp.dot(ckvn, wkn_ref[...], preferred_element_type=F32)
    vt = lax.dot_general(wvt_ref[...], ckvn, (((1,), (1,)), ((), ())), preferred_element_type=F32)
    extra = MLA_VT_ROWS - MLA_V
    ones_row = (lax.broadcasted_iota(jnp.int32, (extra, vt.shape[1]), 0) == 0).astype(BF16)
    for h in range(MLA_HEADS):
        vt_ref[0, h, MLA_V:MLA_VT_ROWS, :] = ones_row
        hs = slice(h * LANES, (h + 1) * LANES)
        q_ref[0, h, :, 0:LANES] = (qn[:, hs] * scale).astype(BF16)
        q_ref[0, h, :, LANES:MLA_QK] = (rope(qp[:, hs]) * scale).astype(BF16)
        k_ref[0, h, :, 0:LANES] = kn[:, hs].astype(BF16)
        k_ref[0, h, :, LANES:MLA_QK] = kpe_r
        vt_ref[0, h, 0:MLA_V, :] = vt[h * MLA_V:(h + 1) * MLA_V, :].astype(BF16)


def _mla_proj(xb, w1, wz, wqn, wqp, wkn, wvt, qg, kvg, cos, sa, sb, scale, batch, seq, tm=256):
    dm = xb.shape[1]
    row = lambda b, i: (b, i, 0)
    fixed = lambda b, i: (0, 0)
    wspec = lambda w: pl.BlockSpec(w.shape, fixed, pipeline_mode=pl.Buffered(1))
    tab = lambda a: a.reshape(batch, seq, LANES)
    q, k, vt, z = pl.pallas_call(
        functools.partial(_mla_proj_kernel, scale=scale),
        grid=(batch, seq // tm),
        in_specs=[pl.BlockSpec((1, tm, dm), row), wspec(w1), wspec(wz), wspec(wqn), wspec(wqp), wspec(wkn),
                  wspec(wvt), wspec(qg), wspec(kvg)] + [pl.BlockSpec((1, tm, LANES), row)] * 3,
        out_specs=[pl.BlockSpec((1, MLA_HEADS, tm, MLA_QK), lambda b, i: (b, 0, i, 0)),
                   pl.BlockSpec((1, MLA_HEADS, tm, MLA_QK), lambda b, i: (b, 0, i, 0)),
                   pl.BlockSpec((1, MLA_HEADS, MLA_VT_ROWS, tm), lambda b, i: (b, 0, 0, i)),
                   pl.BlockSpec((1, tm, MLA_WIDTH), row)],
        out_shape=[jax.ShapeDtypeStruct((batch, MLA_HEADS, seq, MLA_QK), BF16),
                   jax.ShapeDtypeStruct((batch, MLA_HEADS, seq, MLA_QK), BF16),
                   jax.ShapeDtypeStruct((batch, MLA_HEADS, MLA_VT_ROWS, seq), BF16),
                   jax.ShapeDtypeStruct((batch, seq, MLA_WIDTH), BF16)],
        compiler_params=_params(("arbitrary",) * 2, 56),
        name="mla_projections",
    )(xb.reshape(batch, seq, dm), w1, wz, wqn, wqp, wkn, wvt, qg, kvg, tab(cos), tab(sa), tab(sb))
    return q, k, vt, z.reshape(batch * seq, MLA_WIDTH)


def _mla_flash_kernel(qi_ref, ki_ref, q_ref, k_ref, vt_ref, z_ref, o_ref, s_buf, smax_buf, m_scr, acc_scr, *,
                      heads):
    t = pl.program_id(2)
    qi = qi_ref[t]
    ki = ki_ref[t]
    tq = q_ref.shape[2]
    tk = k_ref.shape[2]

    @pl.when(ki == 0)
    def _():
        m_scr[...] = jnp.full(m_scr.shape, -jnp.inf, F32)
        acc_scr[...] = jnp.zeros(acc_scr.shape, F32)

    qc = tq // MLA_QUERY_CHUNKS

    def scores(h, c, masked):
        cs = slice(c * qc, (c + 1) * qc)
        nk = (c + 1) * qc if masked else tk
        s = lax.dot_general(k_ref[0, h, 0:nk, :], q_ref[0, h, cs, :], (((1,), (1,)), ((), ())),
                            preferred_element_type=F32)
        if masked:
            key = lax.broadcasted_iota(jnp.int32, (nk, qc), 0)
            qry = lax.broadcasted_iota(jnp.int32, (nk, qc), 1) + c * qc
            s = jnp.where(qry >= key, s, -jnp.inf)
        s_buf[h % 2, 0:nk, cs] = s
        smax_buf[h % 2, :, cs] = jnp.max(s, axis=0, keepdims=True)

    def softmax(h, c, masked):
        cs = slice(c * qc, (c + 1) * qc)
        nk = (c + 1) * qc if masked else tk
        m_prev = m_scr[h, :, cs]
        m_new = jnp.maximum(m_prev, smax_buf[h % 2, :, cs])
        m_scr[h, :, cs] = m_new
        return jnp.exp2(m_prev - m_new), jnp.exp2(s_buf[h % 2, 0:nk, cs] - m_new).astype(BF16)

    def pv(h, c, masked, alpha, p):
        cs = slice(c * qc, (c + 1) * qc)
        nk = (c + 1) * qc if masked else tk
        acc_scr[h, :, cs] = alpha * acc_scr[h, :, cs] + jnp.dot(
            vt_ref[0, h, :, 0:nk], p, preferred_element_type=F32)

    def update(masked):
        for c in range(MLA_QUERY_CHUNKS):
            scores(0, c, masked)
        for h in range(heads):
            for c in range(MLA_QUERY_CHUNKS):
                alpha, p = softmax(h, c, masked)
                if h + 1 < heads:
                    scores(h + 1, c, masked)
                pv(h, c, masked, alpha, p)

    @pl.when(ki < qi)
    def _():
        update(False)

    @pl.when(ki == qi)
    def _():
        update(True)
        for h in range(heads):
            ln = slice(h * MLA_V, (h + 1) * MLA_V)
            zz = z_ref[0, :, ln].astype(F32)
            acc = acc_scr[h]
            o = (acc[:MLA_V] * (1.0 / acc[MLA_V:MLA_V + 1])).T
            o_ref[0, :, ln] = (o * (zz / (1.0 + jnp.exp(-zz)))).astype(o_ref.dtype)


def _mla_flash(q, k, vt, z, batch, seq):
    heads = MLA_HEADS_PER_STEP
    tile = MLA_TILE
    nt = seq // tile
    qi = np.concatenate([np.full(i + 1, i, np.int32) for i in range(nt)])
    ki = np.concatenate([np.arange(i + 1, dtype=np.int32) for i in range(nt)])
    v_w = heads * MLA_V
    qmap = lambda b, h, t, qi_r, ki_r: (b, h, qi_r[t], 0)
    kmap = lambda b, h, t, qi_r, ki_r: (b, h, ki_r[t], 0)
    vmap = lambda b, h, t, qi_r, ki_r: (b, h, 0, ki_r[t])
    zmap = lambda b, h, t, qi_r, ki_r: (b, qi_r[t], h)
    grid_spec = pltpu.PrefetchScalarGridSpec(
        num_scalar_prefetch=2,
        grid=(batch, MLA_HEADS // heads, len(qi)),
        in_specs=[pl.BlockSpec((1, heads, tile, MLA_QK), qmap), pl.BlockSpec((1, heads, tile, MLA_QK), kmap),
                  pl.BlockSpec((1, heads, MLA_VT_ROWS, tile), vmap), pl.BlockSpec((1, tile, v_w), zmap)],
        out_specs=pl.BlockSpec((1, tile, v_w), zmap),
        scratch_shapes=[pltpu.VMEM((2, tile, tile), F32), pltpu.VMEM((2, 1, tile), F32),
                        pltpu.VMEM((heads, 1, tile), F32), pltpu.VMEM((heads, MLA_VT_ROWS, tile), F32)],
    )
    out = pl.pallas_call(
        functools.partial(_mla_flash_kernel, heads=heads),
        grid_spec=grid_spec,
        out_shape=jax.ShapeDtypeStruct((batch, seq, MLA_WIDTH), BF16),
        compiler_params=_params(("arbitrary",) * 3, 56),
        name="mla_flash_attention",
    )(jnp.asarray(qi), jnp.asarray(ki), q, k, vt, z.reshape(batch, seq, MLA_WIDTH))
    return out.reshape(batch * seq, MLA_WIDTH)


def _inv_freq(dim):
    return 1.0 / (ROPE_THETA ** (jnp.arange(0, dim, 2, dtype=F32) / dim))


def kernel(x, positions, dsa_w_in, dsa_w_out, mla_w_in, mla_q_norm, mla_w_uq, mla_kv_norm, mla_w_ukv,
           mla_w_out, ln_g, ln_b):
    batch, seq, dm = x.shape
    tokens = batch * seq
    half = DSA_HEAD_DIM // 2
    dils = tuple(d for _, d in DSA_PATTERNS)

    inv_a = _inv_freq(DSA_HEAD_DIM)
    inv_row_a = jnp.concatenate([inv_a, inv_a])[None, :]
    mult_a = jnp.stack([jnp.ones((LANES,), F32),
                        jnp.concatenate([-jnp.ones((half,), F32), jnp.ones((half,), F32)])])
    pos_col = positions.reshape(tokens, 1).astype(F32)
    table_nat = tuple(_rope_tables(pos_col, inv_row_a, mult_a, ("cos", "sin")))
    perm_dils = tuple(d for d in dils if d != 1)
    tables_a = dict(zip(perm_dils, _table_perm(table_nat, perm_dils, batch, seq)))
    tables_a[1] = table_nat

    inv_b = _inv_freq(MLA_ROPE)
    hb = MLA_ROPE // 2
    zeros_h = jnp.zeros((hb,), F32)
    ones_h = jnp.ones((hb,), F32)
    pad = jnp.zeros((LANES - MLA_ROPE,), F32)
    inv_row_b = jnp.concatenate([inv_b, inv_b, pad])[None, :]
    mult_b = jnp.stack([jnp.concatenate([ones_h, ones_h, pad]),
                        jnp.concatenate([zeros_h, ones_h, pad]),
                        jnp.concatenate([-ones_h, zeros_h, pad])])
    cos_b, sa_b, sb_b = _rope_tables(pos_col, inv_row_b, mult_b, ("cos", "sin", "sin"))

    x2 = x.reshape(tokens, dm)
    xb = None
    qkv_w = 3 * DSA_WIDTH
    for layer in range(DEPTH):
        j = layer // N_MIXERS
        x3 = x2.reshape(batch, seq, dm)
        if layer % N_MIXERS == 0:
            w_in = dsa_w_in[j].astype(BF16)
            if xb is None:
                xbs = dict(zip(dils, _cast_perm(x3, dils)))
            else:
                rest = tuple(d for d in dils if d != 1)
                xbs = dict(zip(rest, _cast_perm(x3, rest)))
                xbs[1] = xb
            hs = []
            q_scale = DSA_HEAD_DIM ** -0.5 * math.log2(math.e)
            for g, d in enumerate(dils):
                cos, sin = tables_a[d]
                hs.append((_mm_rope(xbs[d], w_in, g * qkv_w, cos, sin, q_scale),
                           _mm(xbs[d], w_in, g * qkv_w + 2 * DSA_WIDTH, DSA_WIDTH)))
            z = _mm(xbs[1], w_in, DSA_GROUPS * qkv_w)
            y = _dsa_attention(hs, z, batch, seq)
            w_out = dsa_w_out[j].astype(BF16)
        else:
            if xb is None:
                xb = _cast_perm(x3, (1,))[0]
            w_in = mla_w_in[j]
            o2 = MLA_Q_RANK + MLA_KV_RANK
            o3 = o2 + MLA_ROPE
            w1 = jnp.pad(w_in[:, :o3], ((0, 0), (0, LANES - MLA_ROPE))).astype(BF16)
            wz = w_in[:, o3:].astype(BF16)
            wq = mla_w_uq[j].reshape(MLA_Q_RANK, MLA_HEADS, MLA_NOPE + MLA_ROPE)
            wqn = wq[:, :, :MLA_NOPE].reshape(MLA_Q_RANK, MLA_HEADS * MLA_NOPE).astype(BF16)
            wqp = jnp.pad(wq[:, :, MLA_NOPE:], ((0, 0), (0, 0), (0, LANES - MLA_ROPE))).reshape(
                MLA_Q_RANK, MLA_HEADS * LANES).astype(BF16)
            wkv = mla_w_ukv[j].reshape(MLA_KV_RANK, MLA_HEADS, MLA_NOPE + MLA_V)
            wkn = wkv[:, :, :MLA_NOPE].reshape(MLA_KV_RANK, MLA_HEADS * MLA_NOPE).astype(BF16)
            wvt = wkv[:, :, MLA_NOPE:].reshape(MLA_KV_RANK, MLA_HEADS * MLA_V).T.astype(BF16)
            q_scale = (MLA_NOPE + MLA_ROPE) ** -0.5 * math.log2(math.e)
            q, k, vt, z = _mla_proj(xb, w1, wz, wqn, wqp, wkn, wvt, mla_q_norm[j][None, :],
                                    mla_kv_norm[j][None, :], cos_b, sa_b, sb_b, q_scale, batch, seq)
            y = _mla_flash(q, k, vt, z, batch, seq)
            w_out = mla_w_out[j].astype(BF16)
        x2, xb = _out_ln(y, w_out, x2, ln_g[layer][None, :], ln_b[layer][None, :])
    return x2.reshape(batch, seq, dm)
```
